```python
import math
import jax, jax.numpy as jnp
from jax import lax
import numpy as np

D_MODEL = 1024
BATCH = 8
SEQ = 4096
DEPTH = 1

N_Q_HEADS = 16
N_KV_HEADS = 4
HEAD_DIM = 64
Q_PER_KV = N_Q_HEADS // N_KV_HEADS
ATTN_WIDTH = N_Q_HEADS * HEAD_DIM
KV_WIDTH = N_KV_HEADS * HEAD_DIM
WINDOW = 128
ATTN_BLOCK = 128
ROT_DIM = HEAD_DIM // 4
ROPE_THETA = 500000.0

SSD_WIDTH = 2 * D_MODEL
SSD_HEAD_DIM = 64
SSD_HEADS = SSD_WIDTH // SSD_HEAD_DIM
SSD_GROUPS = 4
SSD_HEADS_PER_GROUP = SSD_HEADS // SSD_GROUPS
SSD_STATE = 128
SSD_CONV = 4
SSD_CHUNK = 128
CONV_DIM = SSD_WIDTH + 2 * SSD_GROUPS * SSD_STATE

N_BRANCHES = 2
GATE_WIDTH = N_BRANCHES * D_MODEL

IN_SPLITS = [ATTN_WIDTH, ATTN_WIDTH + KV_WIDTH, ATTN_WIDTH + 2 * KV_WIDTH,
             ATTN_WIDTH + 2 * KV_WIDTH + SSD_WIDTH,
             ATTN_WIDTH + 2 * KV_WIDTH + SSD_WIDTH + CONV_DIM,
             ATTN_WIDTH + 2 * KV_WIDTH + SSD_WIDTH + CONV_DIM + SSD_HEADS]
IN_DIM = IN_SPLITS[-1] + GATE_WIDTH

PEER_HEADS = 8
PEER_N_KEYS = 128
PEER_N_EXPERTS = PEER_N_KEYS * PEER_N_KEYS
PEER_TOPK = 16
PEER_QUERY_DIM = 256
PEER_HALF = PEER_QUERY_DIM // 2
PEER_TOKEN_BLOCK = 128

EPS = 1e-6

kernel_name = "hybrid_swa_ssd_peer_block"


def rms_norm(x, w):
    xf = x.astype(jnp.float32)
    y = xf * lax.rsqrt(jnp.mean(xf * xf, axis=-1, keepdims=True) + EPS)
    return (y * w.astype(jnp.float32)).astype(x.dtype)


def partial_rope(t, positions):
    half = ROT_DIM // 2
    inv_freq = ROPE_THETA ** (-jnp.arange(0, ROT_DIM, 2, dtype=jnp.float32) / ROT_DIM)
    ang = positions.astype(jnp.float32)[..., None] * inv_freq
    cos = jnp.cos(ang)[:, :, None, :]
    sin = jnp.sin(ang)[:, :, None, :]
    tf = t.astype(jnp.float32)
    x1 = tf[..., :half]
    x2 = tf[..., half:ROT_DIM]
    rot = jnp.concatenate([x1 * cos - x2 * sin, x2 * cos + x1 * sin], axis=-1)
    return jnp.concatenate([rot.astype(t.dtype), t[..., ROT_DIM:]], axis=-1)


def sliding_window_attention(q, k, v, sinks):
    B, S = q.shape[0], q.shape[1]
    Q = ATTN_BLOCK
    nb = S // Q
    qb = jnp.moveaxis(q.reshape(B, nb, Q, N_KV_HEADS, Q_PER_KV, HEAD_DIM), 1, 0)

    def windows(t):
        tb = t.reshape(B, nb, Q, N_KV_HEADS, HEAD_DIM)
        prev = jnp.concatenate([jnp.zeros_like(tb[:, :1]), tb[:, :-1]], axis=1)
        return jnp.moveaxis(jnp.concatenate([prev, tb], axis=2), 1, 0)

    kw = windows(k)
    vw = windows(v)
    sink = sinks.astype(jnp.float32).reshape(N_KV_HEADS, Q_PER_KV)[None, :, :, None, None]
    qi = jnp.arange(Q)[:, None]
    kj = jnp.arange(2 * Q)[None, :]
    rel = Q + qi - kj
    band = (rel >= 0) & (rel < WINDOW)
    scale = HEAD_DIM ** -0.5

    def block(args):
        qblk, kblk, vblk, bidx = args
        s = jnp.einsum("bqhgd,bkhd->bhgqk", qblk, kblk).astype(jnp.float32) * scale
        valid = band & (bidx * Q + kj - Q >= 0)
        s = jnp.where(valid, s, -jnp.inf)
        m = jnp.maximum(jnp.max(s, axis=-1, keepdims=True), sink)
        p = jnp.exp(s - m)
        p = p / (jnp.sum(p, axis=-1, keepdims=True) + jnp.exp(sink - m))
        return jnp.einsum("bhgqk,bkhd->bqhgd", p.astype(vblk.dtype), vblk)

    out = lax.map(block, (qb, kw, vw, jnp.arange(nb)))
    return jnp.moveaxis(out, 0, 1).reshape(B, S, ATTN_WIDTH)


def causal_depthwise_conv(u, w, b):
    C = u.shape[-1]
    out = lax.conv_general_dilated(
        u, w[:, None, :].astype(u.dtype), window_strides=(1,),
        padding=[(SSD_CONV - 1, 0)], dimension_numbers=("NWC", "WIO", "NWC"),
        feature_group_count=C)
    return out + b.astype(u.dtype)


def ssd_chunked_scan(xdt, dA, Bm, Cm):
    B, S = xdt.shape[0], xdt.shape[1]
    L = SSD_CHUNK
    nc = S // L
    G, R, P, N = SSD_GROUPS, SSD_HEADS_PER_GROUP, SSD_HEAD_DIM, SSD_STATE
    x_c = jnp.moveaxis(xdt.reshape(B, nc, L, G, R, P), 1, 0)
    a_c = jnp.moveaxis(dA.reshape(B, nc, L, G, R), 1, 0)
    b_c = jnp.moveaxis(Bm.reshape(B, nc, L, G, N), 1, 0)
    c_c = jnp.moveaxis(Cm.reshape(B, nc, L, G, N), 1, 0)
    causal = jnp.tril(jnp.ones((L, L), dtype=bool))[None, :, :, None, None]

    def step(state, inp):
        xc, ac, bc, cc = inp
        acum = jnp.cumsum(ac, axis=1)
        seg = acum[:, :, None] - acum[:, None, :]
        decay = jnp.exp(jnp.where(causal, seg, -jnp.inf))
        cb = jnp.einsum("btgn,bsgn->btsg", cc, bc)
        y = jnp.einsum("btsgr,bsgrp->btgrp", cb[..., None] * decay, xc)
        y = y + jnp.einsum("btgn,bgrpn->btgrp", cc, state) * jnp.exp(acum)[..., None]
        total = acum[:, -1]
        w = jnp.exp(total[:, None] - acum)
        state = state * jnp.exp(total)[..., None, None] + jnp.einsum(
            "bsgn,bsgrp->bgrpn", bc, w[..., None] * xc)
        return state, y

    state0 = jnp.zeros((B, G, R, P, N), dtype=jnp.float32)
    _, ys = lax.scan(step, state0, (x_c, a_c, b_c, c_c))
    return jnp.moveaxis(ys, 0, 1).reshape(B, S, SSD_HEADS, SSD_HEAD_DIM)


def peer_ffn(h, wq, pkeys, u_tab, v_tab):
    B, S, D = h.shape
    T = B * S
    hf = h.reshape(T, D)
    q = (hf @ wq).reshape(T, PEER_HEADS, 2, PEER_HALF)
    s = jnp.einsum("thcd,hckd->thck", q, pkeys).astype(jnp.float32)
    v1, i1 = lax.top_k(s[:, :, 0], PEER_TOPK)
    v2, i2 = lax.top_k(s[:, :, 1], PEER_TOPK)
    cand = (v1[..., :, None] + v2[..., None, :]).reshape(T, PEER_HEADS, PEER_TOPK * PEER_TOPK)
    sc, flat = lax.top_k(cand, PEER_TOPK)
    e1 = jnp.take_along_axis(i1, flat // PEER_TOPK, axis=-1)
    e2 = jnp.take_along_axis(i2, flat % PEER_TOPK, axis=-1)
    idx = e1 * PEER_N_KEYS + e2
    g = jax.nn.softmax(sc, axis=-1).astype(h.dtype)
    nblk = T // PEER_TOKEN_BLOCK
    HK = PEER_HEADS * PEER_TOPK
    hb = hf.reshape(nblk, PEER_TOKEN_BLOCK, D)
    ib = idx.reshape(nblk, PEER_TOKEN_BLOCK, HK)
    gb = g.reshape(nblk, PEER_TOKEN_BLOCK, HK)

    def block(args):
        hx, ix, gx = args
        ue = jnp.take(u_tab, ix, axis=0)
        a = jnp.einsum("td,tkd->tk", hx, ue)
        ve = jnp.take(v_tab, ix, axis=0)
        return jnp.einsum("tk,tkd->td", gx * jax.nn.gelu(a, approximate=False), ve)

    out = lax.map(block, (hb, ib, gb))
    return out.reshape(B, S, D)


def setup_inputs(seed: int = 0) -> dict:
    key = jax.random.key(seed)
    ks = jax.random.split(key, 20)
    f32 = jnp.float32
    nrm = lambda k, shape, sc: jax.random.normal(k, shape, f32) * sc
    x = jax.random.normal(ks[0], (BATCH, SEQ, D_MODEL), f32)
    positions = jnp.broadcast_to(jnp.arange(SEQ, dtype=jnp.int32), (BATCH, SEQ))
    dt = jnp.exp(jax.random.uniform(ks[7], (DEPTH, SSD_HEADS), f32,
                                    minval=math.log(1e-3), maxval=math.log(1e-1)))
    dt_bias = dt + jnp.log(-jnp.expm1(-dt))
    a_log = jnp.log(jax.random.uniform(ks[8], (DEPTH, SSD_HEADS), f32, minval=1.0, maxval=16.0))
    return {
        "x": x,
        "positions": positions,
        "norm_mix_w": 1.0 + nrm(ks[1], (DEPTH, D_MODEL), 0.02),
        "w_in": nrm(ks[2], (DEPTH, D_MODEL, IN_DIM), D_MODEL ** -0.5),
        "attn_sinks": nrm(ks[3], (DEPTH, N_Q_HEADS), 0.5),
        "conv_w": nrm(ks[4], (DEPTH, SSD_CONV, CONV_DIM), SSD_CONV ** -0.5),
        "conv_b": nrm(ks[5], (DEPTH, CONV_DIM), 0.01),
        "dt_bias": dt_bias,
        "a_log": a_log,
        "d_skip": 1.0 + nrm(ks[6], (DEPTH, SSD_HEADS), 0.02),
        "ssd_norm_w": 1.0 + nrm(ks[9], (DEPTH, SSD_WIDTH), 0.02),
        "w_attn_o": nrm(ks[10], (DEPTH, ATTN_WIDTH, D_MODEL), ATTN_WIDTH ** -0.5),
        "w_ssd_o": nrm(ks[11], (DEPTH, SSD_WIDTH, D_MODEL), SSD_WIDTH ** -0.5),
        "w_out": nrm(ks[12], (DEPTH, D_MODEL, D_MODEL), D_MODEL ** -0.5),
        "norm_ffn_w": 1.0 + nrm(ks[13], (DEPTH, D_MODEL), 0.02),
        "peer_wq": nrm(ks[14], (DEPTH, D_MODEL, PEER_HEADS * PEER_QUERY_DIM), D_MODEL ** -0.5),
        "peer_keys": nrm(ks[15], (DEPTH, PEER_HEADS, 2, PEER_N_KEYS, PEER_HALF), PEER_HALF ** -0.5),
        "peer_u": nrm(ks[16], (DEPTH, PEER_N_EXPERTS, D_MODEL), D_MODEL ** -0.5),
        "peer_v": nrm(ks[17], (DEPTH, PEER_N_EXPERTS, D_MODEL), PEER_HEADS ** -0.5),
        "norm_final_w": 1.0 + nrm(ks[18], (D_MODEL,), 0.02),
    }


def reference(x, positions, norm_mix_w, w_in, attn_sinks, conv_w, conv_b, dt_bias, a_log,
              d_skip, ssd_norm_w, w_attn_o, w_ssd_o, w_out, norm_ffn_w, peer_wq, peer_keys,
              peer_u, peer_v, norm_final_w):
    B, S, _ = x.shape
    f32 = jnp.float32
    for l in range(DEPTH):
        h = rms_norm(x, norm_mix_w[l])
        proj = h @ w_in[l]
        q, k, v, z, xbc, dt_raw, gates = jnp.split(proj, IN_SPLITS, axis=-1)

        q = partial_rope(q.reshape(B, S, N_Q_HEADS, HEAD_DIM), positions)
        k = partial_rope(k.reshape(B, S, N_KV_HEADS, HEAD_DIM), positions)
        v = v.reshape(B, S, N_KV_HEADS, HEAD_DIM)
        attn = sliding_window_attention(q, k, v, attn_sinks[l]) @ w_attn_o[l]

        xbc = jax.nn.silu(causal_depthwise_conv(xbc, conv_w[l], conv_b[l]))
        xs, bm, cm = jnp.split(xbc, [SSD_WIDTH, SSD_WIDTH + SSD_GROUPS * SSD_STATE], axis=-1)
        xs = xs.reshape(B, S, SSD_HEADS, SSD_HEAD_DIM).astype(f32)
        bm = bm.reshape(B, S, SSD_GROUPS, SSD_STATE).astype(f32)
        cm = cm.reshape(B, S, SSD_GROUPS, SSD_STATE).astype(f32)
        dt = jax.nn.softplus(dt_raw.astype(f32) + dt_bias[l].astype(f32))
        a = -jnp.exp(a_log[l].astype(f32))
        y = ssd_chunked_scan(xs * dt[..., None], dt * a, bm, cm)
        y = y + d_skip[l].astype(f32)[:, None] * xs
        y = y.reshape(B, S, SSD_WIDTH) * jax.nn.silu(z.astype(f32))
        y = rms_norm(y.reshape(B, S, SSD_GROUPS, SSD_WIDTH // SSD_GROUPS),
                     ssd_norm_w[l].reshape(SSD_GROUPS, SSD_WIDTH // SSD_GROUPS))
        ssd = y.reshape(B, S, SSD_WIDTH).astype(x.dtype) @ w_ssd_o[l]

        g_attn, g_ssd = jnp.split(jax.nn.sigmoid(gates), N_BRANCHES, axis=-1)
        x = x + (g_attn * attn + g_ssd * ssd) @ w_out[l]

        x = x + peer_ffn(rms_norm(x, norm_ffn_w[l]), peer_wq[l], peer_keys[l], peer_u[l], peer_v[l])
    return rms_norm(x, norm_final_w)
```

```python
import functools
import math

import jax
import jax.numpy as jnp
from jax import lax
from jax.experimental import pallas as pl
from jax.experimental.pallas import tpu as pltpu

F32 = jnp.float32
BF16 = jnp.bfloat16
I32 = jnp.int32

D_MODEL = 1024
N_Q_HEADS = 16
N_KV_HEADS = 4
HEAD_DIM = 64
ATTN_WIDTH = N_Q_HEADS * HEAD_DIM
KV_WIDTH = N_KV_HEADS * HEAD_DIM
WINDOW = 128
ROT_DIM = HEAD_DIM // 4
ROPE_THETA = 500000.0
SSD_WIDTH = 2 * D_MODEL
SSD_HEAD_DIM = 64
SSD_HEADS = SSD_WIDTH // SSD_HEAD_DIM
SSD_GROUPS = 4
SSD_STATE = 128
SSD_CONV = 4
SSD_CHUNK = 128
BC_WIDTH = SSD_GROUPS * SSD_STATE
PEER_HEADS = 8
PEER_N_KEYS = 128
PEER_TOPK = 16
PEER_HALF = 128
PEER_HK = PEER_HEADS * PEER_TOPK
EPS = 1e-6

LANES = 128
VMEM_LIMIT = 56 * 1024 * 1024

COL_Z = 0
COL_XS = COL_Z + SSD_WIDTH
COL_BM = COL_XS + SSD_WIDTH
COL_CM = COL_BM + BC_WIDTH
COL_GA = COL_CM + BC_WIDTH
COL_GS = COL_GA + D_MODEL
COL_Q = COL_GS + D_MODEL
COL_K = COL_Q + ATTN_WIDTH
COL_V = COL_K + 2 * KV_WIDTH
COL_DT = COL_V + 2 * KV_WIDTH
PROJ_TN = 2432
PROJ_COLS = 4 * PROJ_TN
assert COL_DT + LANES <= PROJ_COLS


def _cparams(*sem):
    return pltpu.CompilerParams(dimension_semantics=sem, vmem_limit_bytes=VMEM_LIMIT)


def _rms(x, w):
    return x * lax.rsqrt(jnp.mean(x * x, axis=-1, keepdims=True) + EPS) * w


def _inproj_body(x_ref, nw_ref, w_ref, o_ref):
    h = _rms(x_ref[...], nw_ref[...])
    o_ref[...] = jnp.dot(h.astype(BF16), w_ref[...], preferred_element_type=F32)


def _inproj(xf, norm_w, w_perm, tm=512):
    T = xf.shape[0]
    return pl.pallas_call(
        _inproj_body,
        grid=(PROJ_COLS // PROJ_TN, T // tm),
        in_specs=[
            pl.BlockSpec((tm, D_MODEL), lambda n, m: (m, 0)),
            pl.BlockSpec((1, D_MODEL), lambda n, m: (0, 0)),
            pl.BlockSpec((D_MODEL, PROJ_TN), lambda n, m: (0, n)),
        ],
        out_specs=pl.BlockSpec((tm, PROJ_TN), lambda n, m: (m, n)),
        out_shape=jax.ShapeDtypeStruct((T, PROJ_COLS), F32),
        compiler_params=_cparams("arbitrary", "arbitrary"),
        name="inproj",
    )(xf, norm_w, w_perm)


def _attn_body(q_ref, kc_ref, vc_ref, kp_ref, vp_ref, pos_ref, posp_ref, invf_ref, sgn_ref,
               sink_ref, wo_ref, o_ref, *, tq):
    i = pl.program_id(1)
    lane = lax.broadcasted_iota(I32, (1, LANES), 1)
    lo8 = (lane % HEAD_DIM) < (ROT_DIM // 2)
    mlo = lane < HEAD_DIM

    def rope(t, pos):
        ang = pos.astype(F32) * invf_ref[...]
        cs = jnp.cos(ang)
        sn = jnp.sin(ang) * sgn_ref[...]
        outs = []
        for j in range(t.shape[1] // LANES):
            tj = t[:, j * LANES:(j + 1) * LANES]
            sh = jnp.where(lo8, pltpu.roll(tj, LANES - ROT_DIM // 2, 1), pltpu.roll(tj, ROT_DIM // 2, 1))
            outs.append(tj * cs + sh * sn)
        return jnp.concatenate(outs, axis=1)

    pos = pos_ref[...]
    qrot = rope(q_ref[...], pos)
    kfull = jnp.concatenate([rope(kp_ref[...], posp_ref[...]), rope(kc_ref[...], pos)], axis=0).astype(BF16)
    vfull = jnp.concatenate([vp_ref[...], vc_ref[...]], axis=0).astype(BF16)

    qi = lax.broadcasted_iota(I32, (WINDOW, 2 * WINDOW), 0)
    kj = lax.broadcasted_iota(I32, (WINDOW, 2 * WINDOW), 1)
    rel = WINDOW + qi - kj
    band = (rel >= 0) & (rel < WINDOW)
    rgrp = lax.broadcasted_iota(I32, (4 * WINDOW, 1), 0) // WINDOW
    scale = HEAD_DIM ** -0.5

    for c in range(tq // WINDOW):
        valid = band
        if c == 0:
            valid = band & ((kj >= WINDOW) | (i > 0))
        valid4 = jnp.concatenate([valid] * 4, axis=0)
        tiles = []
        for h in range(N_KV_HEADS):
            kh = kfull[c * WINDOW:c * WINDOW + 2 * WINDOW, h * LANES:(h + 1) * LANES]
            vh = vfull[c * WINDOW:c * WINDOW + 2 * WINDOW, h * LANES:(h + 1) * LANES]
            t0 = qrot[c * WINDOW:(c + 1) * WINDOW, (2 * h) * LANES:(2 * h + 1) * LANES]
            t1 = qrot[c * WINDOW:(c + 1) * WINDOW, (2 * h + 1) * LANES:(2 * h + 2) * LANES]
            qg = jnp.concatenate([jnp.where(mlo, t0, 0.0), jnp.where(mlo, 0.0, t0),
                                  jnp.where(mlo, t1, 0.0), jnp.where(mlo, 0.0, t1)], axis=0).astype(BF16)
            s = lax.dot_general(qg, kh, (((1,), (1,)), ((), ())), preferred_element_type=F32) * scale
            s = jnp.where(valid4, s, -jnp.inf)
            sk = jnp.where(rgrp == 0, sink_ref[4 * h],
                           jnp.where(rgrp == 1, sink_ref[4 * h + 1],
                                     jnp.where(rgrp == 2, sink_ref[4 * h + 2], sink_ref[4 * h + 3])))
            m = jnp.maximum(jnp.max(s, axis=-1, keepdims=True), sk)
            p = jnp.exp(s - m)
            den = jnp.sum(p, axis=-1, keepdims=True) + jnp.exp(sk - m)
            o = jnp.dot(p.astype(BF16), vh, preferred_element_type=F32) / den
            tiles.append(jnp.where(mlo, o[0:WINDOW], o[WINDOW:2 * WINDOW]))
            tiles.append(jnp.where(mlo, o[2 * WINDOW:3 * WINDOW], o[3 * WINDOW:4 * WINDOW]))
        attn = jnp.concatenate(tiles, axis=1).astype(BF16)
        o_ref[c * WINDOW:(c + 1) * WINDOW, :] = jnp.dot(attn, wo_ref[...], preferred_element_type=F32)


def _attention(proj, pos_col, invf, sgn, sinks, w_o, B, S, tq=512):
    T = B * S
    nq = S // tq
    nb = S // WINDOW
    r = tq // WINDOW
    cur = lambda col: (lambda b, i: (b * nq + i, col))
    prev = lambda col: (lambda b, i: (b * nb + jnp.maximum(i * r - 1, 0), col))
    return pl.pallas_call(
        functools.partial(_attn_body, tq=tq),
        grid=(B, nq),
        in_specs=[
            pl.BlockSpec((tq, ATTN_WIDTH), cur(COL_Q // ATTN_WIDTH)),
            pl.BlockSpec((tq, 2 * KV_WIDTH), cur(COL_K // (2 * KV_WIDTH))),
            pl.BlockSpec((tq, 2 * KV_WIDTH), cur(COL_V // (2 * KV_WIDTH))),
            pl.BlockSpec((WINDOW, 2 * KV_WIDTH), prev(COL_K // (2 * KV_WIDTH))),
            pl.BlockSpec((WINDOW, 2 * KV_WIDTH), prev(COL_V // (2 * KV_WIDTH))),
            pl.BlockSpec((tq, 1), cur(0)),
            pl.BlockSpec((WINDOW, 1), prev(0)),
            pl.BlockSpec((1, LANES), lambda b, i: (0, 0)),
            pl.BlockSpec((1, LANES), lambda b, i: (0, 0)),
            pl.BlockSpec(memory_space=pltpu.SMEM),
            pl.BlockSpec((ATTN_WIDTH, D_MODEL), lambda b, i: (0, 0)),
        ],
        out_specs=pl.BlockSpec((tq, D_MODEL), cur(0)),
        out_shape=jax.ShapeDtypeStruct((T, D_MODEL), F32),
        compiler_params=_cparams("arbitrary", "arbitrary"),
        name="attention",
    )(proj, proj, proj, proj, proj, pos_col, pos_col, invf, sgn, sinks, w_o)


def _split3(a):
    hi = a.astype(BF16)
    r1 = a - hi.astype(F32)
    mid = r1.astype(BF16)
    lo = (r1 - mid.astype(F32)).astype(BF16)
    return hi, mid, lo


def _sel_dot(a, e):
    hi, mid, lo = _split3(a)
    d = lambda u: jnp.dot(u, e, preferred_element_type=F32)
    return d(hi) + d(mid) + d(lo)


def _ssd_body(z_ref, xs_ref, bm_ref, cm_ref, dt_ref, cwx_ref, cwb_ref, cwc_ref, cbx_ref, cbb_ref, cbc_ref,
              dtb_ref, alog_ref, dsk_ref, nw_ref, e64_ref, e128_ref, tril_ref, o_ref,
              state, tail_x, tail_b, tail_c):
    L = SSD_CHUNK

    @pl.when(pl.program_id(1) == 0)
    def _():
        state[...] = jnp.zeros_like(state)
        tail_x[...] = jnp.zeros_like(tail_x)
        tail_b[...] = jnp.zeros_like(tail_b)
        tail_c[...] = jnp.zeros_like(tail_c)

    row8 = lax.broadcasted_iota(I32, (8, 1), 0)

    def conv_silu(u, tail_ref, w_ref, b_ref):
        tail = tail_ref[...]
        acc = u * w_ref[SSD_CONV - 1:SSD_CONV, :] + b_ref[...]
        for j in range(1, SSD_CONV):
            ru = pltpu.roll(u, j, 0)
            head = jnp.where(row8 < j, pltpu.roll(tail, j, 0), ru[0:8])
            sh = jnp.concatenate([head, ru[8:]], axis=0)
            acc = acc + sh * w_ref[SSD_CONV - 1 - j:SSD_CONV - j, :]
        tail_ref[...] = u[L - 8:L]
        return acc * jax.nn.sigmoid(acc)

    xs = conv_silu(xs_ref[...], tail_x, cwx_ref, cbx_ref)
    bm = conv_silu(bm_ref[...], tail_b, cwb_ref, cbb_ref)
    cm = conv_silu(cm_ref[...], tail_c, cwc_ref, cbc_ref)

    dt = jax.nn.softplus(dt_ref[...] + dtb_ref[...])
    dA = dt * (-jnp.exp(alog_ref[...]))
    hi, mid, lo = _split3(dA)
    tril = tril_ref[...]
    cs = lambda u: jnp.dot(tril, u, preferred_element_type=F32)
    acum = cs(hi) + cs(mid) + cs(lo)
    acum_t = acum.T
    e64 = e64_ref[...]
    dt_x = _sel_dot(dt, e64)
    acum_x = _sel_dot(acum, e64)
    tot_x = acum_x[L - 1:L, :]
    xdt = xs * dt_x
    xdt_b = xdt.astype(BF16)
    wx_b = (jnp.exp(tot_x - acum_x) * xdt).astype(BF16)
    eac_x = jnp.exp(acum_x)
    etot_x = jnp.exp(tot_x)
    cm_b = cm.astype(BF16)
    bm_b = bm.astype(BF16)

    ti = lax.broadcasted_iota(I32, (L, L), 0)
    si = lax.broadcasted_iota(I32, (L, L), 1)
    causal = ti >= si
    lane = lax.broadcasted_iota(I32, (1, LANES), 1)
    mlo = lane < SSD_HEAD_DIM
    GW = SSD_WIDTH // SSD_GROUPS
    ys = []
    for g in range(SSD_GROUPS):
        cmg = cm_b[:, g * SSD_STATE:(g + 1) * SSD_STATE]
        bmg = bm_b[:, g * SSD_STATE:(g + 1) * SSD_STATE]
        cb = lax.dot_general(cmg, bmg, (((1,), (1,)), ((), ())), preferred_element_type=F32)
        colx = _sel_dot(acum, e128_ref[:, g * 8 * LANES:(g + 1) * 8 * LANES])
        tiles = []
        for j in range(4):
            xt = xdt_b[:, g * GW + j * LANES:g * GW + (j + 1) * LANES]
            acc = None
            for half in range(2):
                r = 2 * j + half
                h = 8 * g + r
                seg = colx[:, r * LANES:(r + 1) * LANES] - acum_t[h:h + 1, :]
                dec = jnp.exp(jnp.where(causal, seg, -jnp.inf))
                mm = (cb * dec).astype(BF16)
                xm = jnp.where(mlo, xt, 0.0) if half == 0 else jnp.where(mlo, 0.0, xt)
                y = jnp.dot(mm, xm.astype(BF16), preferred_element_type=F32)
                acc = y if acc is None else acc + y
            tiles.append(acc)
        y_intra = jnp.concatenate(tiles, axis=1)
        st = state[:, g * GW:(g + 1) * GW]
        y_inter = jnp.dot(cmg, st.astype(BF16), preferred_element_type=F32) * eac_x[:, g * GW:(g + 1) * GW]
        ys.append(y_intra + y_inter)
        bmt = bm[:, g * SSD_STATE:(g + 1) * SSD_STATE].T.astype(BF16)
        state[:, g * GW:(g + 1) * GW] = st * etot_x[:, g * GW:(g + 1) * GW] + jnp.dot(
            bmt, wx_b[:, g * GW:(g + 1) * GW], preferred_element_type=F32)

    y = jnp.concatenate(ys, axis=1) + dsk_ref[...] * xs
    z = z_ref[...]
    y = y * (z * jax.nn.sigmoid(z))
    outs = []
    for g in range(SSD_GROUPS):
        outs.append(_rms(y[:, g * GW:(g + 1) * GW], nw_ref[:, g * GW:(g + 1) * GW]))
    o_ref[...] = jnp.concatenate(outs, axis=1).astype(BF16)


def _ssd(proj, consts, B, S):
    T = B * S
    L = SSD_CHUNK
    nc = S // L
    blk = lambda w, col: pl.BlockSpec((L, w), lambda b, c: (b * nc + c, col))
    full = lambda a: pl.BlockSpec(a.shape, lambda b, c: (0,) * a.ndim)
    return pl.pallas_call(
        _ssd_body,
        grid=(B, nc),
        in_specs=[
            blk(SSD_WIDTH, COL_Z // SSD_WIDTH),
            blk(SSD_WIDTH, COL_XS // SSD_WIDTH),
            blk(BC_WIDTH, COL_BM // BC_WIDTH),
            blk(BC_WIDTH, COL_CM // BC_WIDTH),
            blk(LANES, COL_DT // LANES),
        ] + [full(a) for a in consts],
        out_specs=pl.BlockSpec((L, SSD_WIDTH), lambda b, c: (b * nc + c, 0)),
        out_shape=jax.ShapeDtypeStruct((T, SSD_WIDTH), BF16),
        scratch_shapes=[
            pltpu.VMEM((SSD_STATE, SSD_WIDTH), F32),
            pltpu.VMEM((8, SSD_WIDTH), F32),
            pltpu.VMEM((8, BC_WIDTH), F32),
            pltpu.VMEM((8, BC_WIDTH), F32),
        ],
        compiler_params=_cparams("arbitrary", "arbitrary"),
        name="ssd",
    )(proj, proj, proj, proj, proj, *consts)


def _merge_body(x_ref, at_ref, yn_ref, ga_ref, gs_ref, wso_ref, wout_ref, o_ref):
    ssd = jnp.dot(yn_ref[...], wso_ref[...], preferred_element_type=F32)
    m = jax.nn.sigmoid(ga_ref[...]) * at_ref[...] + jax.nn.sigmoid(gs_ref[...]) * ssd
    o_ref[...] = x_ref[...] + jnp.dot(m.astype(BF16), wout_ref[...], preferred_element_type=F32)


def _merge(xf, attn_o, yn, proj, w_ssd_o, w_out, tm=512):
    T = xf.shape[0]
    row = lambda w, col: pl.BlockSpec((tm, w), lambda m: (m, col))
    return pl.pallas_call(
        _merge_body,
        grid=(T // tm,),
        in_specs=[
            row(D_MODEL, 0), row(D_MODEL, 0), row(SSD_WIDTH, 0),
            row(D_MODEL, COL_GA // D_MODEL), row(D_MODEL, COL_GS // D_MODEL),
            pl.BlockSpec((SSD_WIDTH, D_MODEL), lambda m: (0, 0)),
            pl.BlockSpec((D_MODEL, D_MODEL), lambda m: (0, 0)),
        ],
        out_specs=row(D_MODEL, 0),
        out_shape=jax.ShapeDtypeStruct((T, D_MODEL), F32),
        compiler_params=_cparams("arbitrary"),
        name="merge",
    )(xf, attn_o, yn, proj, proj, w_ssd_o, w_out)


def _top16(s, payload=None):
    n = s.shape[0]
    iota = lax.broadcasted_iota(I32, s.shape, 0)
    vals, sel = [], []
    for _ in range(PEER_TOPK):
        m = jnp.max(s, axis=0, keepdims=True)
        am = jnp.min(jnp.where(s == m, iota, n), axis=0, keepdims=True)
        hit = iota == am
        vals.append(m)
        sel.append(am if payload is None else jnp.max(jnp.where(hit, payload, -1), axis=0, keepdims=True))
        s = jnp.where(hit, -jnp.inf, s)
    return jnp.concatenate(vals, axis=0), jnp.concatenate(sel, axis=0)


def _route_body(x_ref, nw_ref, wqt_ref, keys_ref, idx_ref, g_ref, qt_scr, it_scr, gt_scr):
    hn = _rms(x_ref[...], nw_ref[...]).astype(BF16)
    qt = lax.dot_general(wqt_ref[...], hn, (((1,), (1,)), ((), ())), preferred_element_type=F32)
    qt_scr[...] = qt.astype(BF16)

    def head(h, carry):
        r1 = pl.multiple_of(h * 2 * PEER_HALF, PEER_HALF)
        r2 = pl.multiple_of(h * 2 * PEER_HALF + PEER_HALF, PEER_HALF)
        s1 = jnp.dot(keys_ref[2 * h], qt_scr[pl.ds(r1, PEER_HALF), :], preferred_element_type=F32)
        s2 = jnp.dot(keys_ref[2 * h + 1], qt_scr[pl.ds(r2, PEER_HALF), :], preferred_element_type=F32)
        v1, i1 = _top16(s1)
        v2, i2 = _top16(s2)
        cand = jnp.concatenate([v1[i:i + 1] + v2 for i in range(PEER_TOPK)], axis=0)
        ecand = jnp.concatenate([i1[i:i + 1] * PEER_N_KEYS + i2 for i in range(PEER_TOPK)], axis=0)
        sc, e = _top16(cand, ecand)
        p = jnp.exp(sc - jnp.max(sc, axis=0, keepdims=True))
        gate = p / jnp.sum(p, axis=0, keepdims=True)
        o = pl.multiple_of(h * PEER_TOPK, PEER_TOPK)
        it_scr[pl.ds(o, PEER_TOPK), :] = e.astype(F32)
        gt_scr[pl.ds(o, PEER_TOPK), :] = gate
        return carry

    lax.fori_loop(0, PEER_HEADS, head, 0)
    idx_ref[...] = it_scr[...].T.astype(I32)
    g_ref[...] = gt_scr[...].T


def _route(x1, norm_w, wq_t, keys, tb=128):
    T = x1.shape[0]
    return pl.pallas_call(
        _route_body,
        grid=(T // tb,),
        in_specs=[
            pl.BlockSpec((tb, D_MODEL), lambda m: (m, 0)),
            pl.BlockSpec((1, D_MODEL), lambda m: (0, 0)),
            pl.BlockSpec(wq_t.shape, lambda m: (0, 0)),
            pl.BlockSpec(keys.shape, lambda m: (0, 0, 0)),
        ],
        out_specs=[pl.BlockSpec((tb, PEER_HK), lambda m: (m, 0)), pl.BlockSpec((tb, PEER_HK), lambda m: (m, 0))],
        out_shape=[jax.ShapeDtypeStruct((T, PEER_HK), I32), jax.ShapeDtypeStruct((T, PEER_HK), F32)],
        scratch_shapes=[
            pltpu.VMEM((wq_t.shape[0], tb), BF16),
            pltpu.VMEM((PEER_HK, tb), F32),
            pltpu.VMEM((PEER_HK, tb), F32),
        ],
        compiler_params=_cparams("arbitrary"),
        name="route",
    )(x1, norm_w, wq_t, keys)


def _peer_body(idx0_ref, idxn_ref, x_ref, g_ref, nfw_ref, nlw_ref, uv_ref, o_ref, gbuf, sem, acc, *, tb):
    i = pl.program_id(0)
    n = pl.num_programs(0)
    rows = tb * PEER_HK

    def issue(idx_ref, slot):
        def per_token(t, carry):
            base = t * PEER_HK
            for k in range(PEER_HK):
                e = idx_ref[t, k]
                pltpu.make_async_copy(uv_ref.at[pl.ds(e, 1), :], gbuf.at[slot, pl.ds(base + k, 1), :],
                                      sem.at[slot]).start(priority=k % 2)
            return carry
        lax.fori_loop(0, tb, per_token, 0)

    @pl.when(i == 0)
    def _():
        issue(idx0_ref, 0)

    @pl.when(i + 1 < n)
    def _():
        issue(idxn_ref, (i + 1) % 2)

    slot = i % 2
    pltpu.make_async_copy(uv_ref.at[pl.ds(0, rows), :], gbuf.at[slot], sem.at[slot]).wait()

    x = x_ref[...]
    hn = _rms(x, nfw_ref[...])
    eye = lax.broadcasted_iota(I32, (PEER_HK, PEER_HK), 0) == lax.broadcasted_iota(I32, (PEER_HK, PEER_HK), 1)
    for t in range(tb):
        u = gbuf[slot, t * PEER_HK:(t + 1) * PEER_HK, 0:D_MODEL]
        a = jnp.sum(u * hn[t:t + 1, :], axis=-1, keepdims=True)
        gcol = jnp.sum(jnp.where(eye, g_ref[t:t + 1, :], 0.0), axis=-1, keepdims=True)
        c = gcol * (0.5 * a * (1.0 + lax.erf(a * (2.0 ** -0.5))))
        v = gbuf[slot, t * PEER_HK:(t + 1) * PEER_HK, D_MODEL:2 * D_MODEL]
        acc[t:t + 1, :] = jnp.sum(c * v, axis=0, keepdims=True)
    o_ref[...] = _rms(x + acc[...], nlw_ref[...])


def _peer(idx, gates, x1, norm_ffn_w, norm_final_w, uv, tb=16):
    T = x1.shape[0]
    nblk = T // tb
    return pl.pallas_call(
        functools.partial(_peer_body, tb=tb),
        grid=(nblk,),
        in_specs=[
            pl.BlockSpec((tb, PEER_HK), lambda m: (m, 0), memory_space=pltpu.SMEM),
            pl.BlockSpec((tb, PEER_HK), lambda m: (jnp.minimum(m + 1, nblk - 1), 0), memory_space=pltpu.SMEM),
            pl.BlockSpec((tb, D_MODEL), lambda m: (m, 0)),
            pl.BlockSpec((tb, PEER_HK), lambda m: (m, 0)),
            pl.BlockSpec((1, D_MODEL), lambda m: (0, 0)),
            pl.BlockSpec((1, D_MODEL), lambda m: (0, 0)),
            pl.BlockSpec(memory_space=pl.ANY),
        ],
        out_specs=pl.BlockSpec((tb, D_MODEL), lambda m: (m, 0)),
        out_shape=jax.ShapeDtypeStruct((T, D_MODEL), F32),
        scratch_shapes=[
            pltpu.VMEM((2, tb * PEER_HK, 2 * D_MODEL), F32),
            pltpu.SemaphoreType.DMA((2,)),
            pltpu.VMEM((tb, D_MODEL), F32),
        ],
        compiler_params=_cparams("arbitrary"),
        name="peer",
    )(idx, idx, x1, gates, norm_ffn_w, norm_final_w, uv)


def _layer_weights(w_in, conv_w, conv_b, dt_bias, a_log, d_skip):
    D = D_MODEL
    q, k, v, z, xbc, dt, gates = jnp.split(
        w_in, [ATTN_WIDTH, ATTN_WIDTH + KV_WIDTH, ATTN_WIDTH + 2 * KV_WIDTH,
               ATTN_WIDTH + 2 * KV_WIDTH + SSD_WIDTH,
               ATTN_WIDTH + 2 * KV_WIDTH + SSD_WIDTH + SSD_WIDTH + 2 * BC_WIDTH,
               ATTN_WIDTH + 2 * KV_WIDTH + SSD_WIDTH + SSD_WIDTH + 2 * BC_WIDTH + SSD_HEADS], axis=1)
    dup = lambda t: jnp.concatenate([t.reshape(D, N_KV_HEADS, HEAD_DIM)] * 2, axis=-1).reshape(D, 2 * KV_WIDTH)
    pad = PROJ_COLS - (COL_DT + SSD_HEADS)
    w_perm = jnp.concatenate([z, xbc, gates, q, dup(k), dup(v), dt, jnp.zeros((D, pad), w_in.dtype)],
                             axis=1).astype(BF16)
    lanes_pad = lambda a: jnp.pad(a.reshape(1, SSD_HEADS), ((0, 0), (0, LANES - SSD_HEADS)))
    ssd_consts = [
        conv_w[:, :SSD_WIDTH], conv_w[:, SSD_WIDTH:SSD_WIDTH + BC_WIDTH], conv_w[:, SSD_WIDTH + BC_WIDTH:],
        conv_b[None, :SSD_WIDTH], conv_b[None, SSD_WIDTH:SSD_WIDTH + BC_WIDTH], conv_b[None, SSD_WIDTH + BC_WIDTH:],
        lanes_pad(dt_bias), lanes_pad(a_log),
        jnp.repeat(d_skip, SSD_HEAD_DIM)[None, :],
    ]
    return w_perm, ssd_consts


def _constants():
    lane = jnp.arange(LANES)
    hl = lane % HEAD_DIM
    half = ROT_DIM // 2
    freq = ROPE_THETA ** (-jnp.arange(0, ROT_DIM, 2, dtype=F32) / ROT_DIM)
    invf = jnp.where(hl < ROT_DIM, freq[hl % half], 0.0).astype(F32)[None, :]
    sgn = jnp.where(hl < half, -1.0, jnp.where(hl < ROT_DIM, 1.0, 0.0)).astype(F32)[None, :]
    hrow = jnp.arange(LANES)[:, None]
    e64 = (hrow == (jnp.arange(SSD_WIDTH)[None, :] // SSD_HEAD_DIM)).astype(BF16)
    e128 = (hrow == (jnp.arange(SSD_HEADS * LANES)[None, :] // LANES)).astype(BF16)
    tril = (jnp.arange(SSD_CHUNK)[:, None] >= jnp.arange(SSD_CHUNK)[None, :]).astype(BF16)
    return invf, sgn, e64, e128, tril


def kernel(x, positions, norm_mix_w, w_in, attn_sinks, conv_w, conv_b, dt_bias, a_log, d_skip, ssd_norm_w,
           w_attn_o, w_ssd_o, w_out, norm_ffn_w, peer_wq, peer_keys, peer_u, peer_v, norm_final_w):
    B, S, D = x.shape
    T = B * S
    depth = w_in.shape[0]
    invf, sgn, e64, e128, tril = _constants()
    pos_col = positions.reshape(T, 1)
    xf = x.reshape(T, D)
    out = None
    for l in range(depth):
        w_perm, ssd_consts = _layer_weights(w_in[l], conv_w[l], conv_b[l], dt_bias[l], a_log[l], d_skip[l])
        proj = _inproj(xf, norm_mix_w[l][None, :], w_perm)
        attn_o = _attention(proj, pos_col, invf, sgn, attn_sinks[l], w_attn_o[l].astype(BF16), B, S)
        yn = _ssd(proj, ssd_consts + [ssd_norm_w[l][None, :], e64, e128, tril], B, S)
        x1 = _merge(xf, attn_o, yn, proj, w_ssd_o[l].astype(BF16), w_out[l].astype(BF16))
        keys = peer_keys[l].reshape(PEER_HEADS * 2, PEER_N_KEYS, PEER_HALF).astype(BF16)
        idx, gates = _route(x1, norm_ffn_w[l][None, :], peer_wq[l].T.astype(BF16), keys)
        uv = jnp.concatenate([peer_u[l], peer_v[l]], axis=1)
        last = l == depth - 1
        nlw = norm_final_w[None, :] if last else jnp.ones((1, D), F32)
        out = _peer(idx, gates, x1, norm_ffn_w[l][None, :], nlw, uv)
        xf = out
    return out.reshape(B, S, D)
```

```python
import functools
import math

import jax
import jax.numpy as jnp
from jax import lax
from jax.experimental import pallas as pl
from jax.experimental.pallas import tpu as pltpu

F32 = jnp.float32
BF16 = jnp.bfloat16
I32 = jnp.int32

D_MODEL = 1024
N_Q_HEADS = 16
N_KV_HEADS = 4
HEAD_DIM = 64
ATTN_WIDTH = N_Q_HEADS * HEAD_DIM
KV_WIDTH = N_KV_HEADS * HEAD_DIM
WINDOW = 128
ROT_DIM = HEAD_DIM // 4
ROPE_THETA = 500000.0
SSD_WIDTH = 2 * D_MODEL
SSD_HEAD_DIM = 64
SSD_HEADS = SSD_WIDTH // SSD_HEAD_DIM
SSD_GROUPS = 4
SSD_STATE = 128
SSD_CONV = 4
SSD_CHUNK = 128
BC_WIDTH = SSD_GROUPS * SSD_STATE
PEER_HEADS = 8
PEER_N_KEYS = 128
PEER_TOPK = 16
PEER_HALF = 128
PEER_HK = PEER_HEADS * PEER_TOPK
EPS = 1e-6

LANES = 128
VMEM_LIMIT = 56 * 1024 * 1024

COL_Z = 0
COL_XS = COL_Z + SSD_WIDTH
COL_BM = COL_XS + SSD_WIDTH
COL_CM = COL_BM + BC_WIDTH
COL_GA = COL_CM + BC_WIDTH
COL_GS = COL_GA + D_MODEL
COL_Q = COL_GS + D_MODEL
COL_K = COL_Q + ATTN_WIDTH
COL_V = COL_K + 2 * KV_WIDTH
COL_DT = COL_V + 2 * KV_WIDTH
PROJ_TN = 2432
PROJ_COLS = 4 * PROJ_TN
assert COL_DT + LANES <= PROJ_COLS


def _cparams(*sem):
    return pltpu.CompilerParams(dimension_semantics=sem, vmem_limit_bytes=VMEM_LIMIT)


def _rms(x, w):
    return x * lax.rsqrt(jnp.mean(x * x, axis=-1, keepdims=True) + EPS) * w


def _inproj_body(x_ref, nw_ref, w_ref, o_ref):
    h = _rms(x_ref[...], nw_ref[...])
    o_ref[...] = jnp.dot(h.astype(BF16), w_ref[...], preferred_element_type=F32)


def _inproj(xf, norm_w, w_perm, tm=512):
    T = xf.shape[0]
    return pl.pallas_call(
        _inproj_body,
        grid=(PROJ_COLS // PROJ_TN, T // tm),
        in_specs=[
            pl.BlockSpec((tm, D_MODEL), lambda n, m: (m, 0)),
            pl.BlockSpec((1, D_MODEL), lambda n, m: (0, 0)),
            pl.BlockSpec((D_MODEL, PROJ_TN), lambda n, m: (0, n)),
        ],
        out_specs=pl.BlockSpec((tm, PROJ_TN), lambda n, m: (m, n)),
        out_shape=jax.ShapeDtypeStruct((T, PROJ_COLS), F32),
        compiler_params=_cparams("arbitrary", "arbitrary"),
        name="inproj",
    )(xf, norm_w, w_perm)


def _attn_body(q_ref, kc_ref, vc_ref, kp_ref, vp_ref, pos_ref, posp_ref, invf_ref, sgn_ref,
               sink_ref, wo_ref, o_ref, *, tq):
    i = pl.program_id(1)
    lane = lax.broadcasted_iota(I32, (1, LANES), 1)
    lo8 = (lane % HEAD_DIM) < (ROT_DIM // 2)
    mlo = lane < HEAD_DIM

    def rope(t, pos):
        ang = pos.astype(F32) * invf_ref[...]
        cs = jnp.cos(ang)
        sn = jnp.sin(ang) * sgn_ref[...]
        outs = []
        for j in range(t.shape[1] // LANES):
            tj = t[:, j * LANES:(j + 1) * LANES]
            sh = jnp.where(lo8, pltpu.roll(tj, LANES - ROT_DIM // 2, 1), pltpu.roll(tj, ROT_DIM // 2, 1))
            outs.append(tj * cs + sh * sn)
        return jnp.concatenate(outs, axis=1)

    pos = pos_ref[...]
    qrot = rope(q_ref[...], pos)
    kfull = jnp.concatenate([rope(kp_ref[...], posp_ref[...]), rope(kc_ref[...], pos)], axis=0).astype(BF16)
    vfull = jnp.concatenate([vp_ref[...], vc_ref[...]], axis=0).astype(BF16)

    qi = lax.broadcasted_iota(I32, (WINDOW, 2 * WINDOW), 0)
    kj = lax.broadcasted_iota(I32, (WINDOW, 2 * WINDOW), 1)
    rel = WINDOW + qi - kj
    band = (rel >= 0) & (rel < WINDOW)
    rgrp = lax.broadcasted_iota(I32, (4 * WINDOW, 1), 0) // WINDOW
    scale = HEAD_DIM ** -0.5

    for c in range(tq // WINDOW):
        valid = band
        if c == 0:
            valid = band & ((kj >= WINDOW) | (i > 0))
        valid4 = jnp.concatenate([valid] * 4, axis=0)
        tiles = []
        for h in range(N_KV_HEADS):
            kh = kfull[c * WINDOW:c * WINDOW + 2 * WINDOW, h * LANES:(h + 1) * LANES]
            vh = vfull[c * WINDOW:c * WINDOW + 2 * WINDOW, h * LANES:(h + 1) * LANES]
            t0 = qrot[c * WINDOW:(c + 1) * WINDOW, (2 * h) * LANES:(2 * h + 1) * LANES]
            t1 = qrot[c * WINDOW:(c + 1) * WINDOW, (2 * h + 1) * LANES:(2 * h + 2) * LANES]
            qg = jnp.concatenate([jnp.where(mlo, t0, 0.0), jnp.where(mlo, 0.0, t0),
                                  jnp.where(mlo, t1, 0.0), jnp.where(mlo, 0.0, t1)], axis=0).astype(BF16)
            s = lax.dot_general(qg, kh, (((1,), (1,)), ((), ())), preferred_element_type=F32) * scale
            s = jnp.where(valid4, s, -jnp.inf)
            sk = jnp.where(rgrp == 0, sink_ref[4 * h],
                           jnp.where(rgrp == 1, sink_ref[4 * h + 1],
                                     jnp.where(rgrp == 2, sink_ref[4 * h + 2], sink_ref[4 * h + 3])))
            m = jnp.maximum(jnp.max(s, axis=-1, keepdims=True), sk)
            p = jnp.exp(s - m)
            den = jnp.sum(p, axis=-1, keepdims=True) + jnp.exp(sk - m)
            o = jnp.dot(p.astype(BF16), vh, preferred_element_type=F32) / den
            tiles.append(jnp.where(mlo, o[0:WINDOW], o[WINDOW:2 * WINDOW]))
            tiles.append(jnp.where(mlo, o[2 * WINDOW:3 * WINDOW], o[3 * WINDOW:4 * WINDOW]))
        attn = jnp.concatenate(tiles, axis=1).astype(BF16)
        o_ref[c * WINDOW:(c + 1) * WINDOW, :] = jnp.dot(attn, wo_ref[...], preferred_element_type=F32)


def _attention(proj, pos_col, invf, sgn, sinks, w_o, B, S, tq=512):
    T = B * S
    nq = S // tq
    nb = S // WINDOW
    r = tq // WINDOW
    cur = lambda col: (lambda b, i: (b * nq + i, col))
    prev = lambda col: (lambda b, i: (b * nb + jnp.maximum(i * r - 1, 0), col))
    return pl.pallas_call(
        functools.partial(_attn_body, tq=tq),
        grid=(B, nq),
        in_specs=[
            pl.BlockSpec((tq, ATTN_WIDTH), cur(COL_Q // ATTN_WIDTH)),
            pl.BlockSpec((tq, 2 * KV_WIDTH), cur(COL_K // (2 * KV_WIDTH))),
            pl.BlockSpec((tq, 2 * KV_WIDTH), cur(COL_V // (2 * KV_WIDTH))),
            pl.BlockSpec((WINDOW, 2 * KV_WIDTH), prev(COL_K // (2 * KV_WIDTH))),
            pl.BlockSpec((WINDOW, 2 * KV_WIDTH), prev(COL_V // (2 * KV_WIDTH))),
            pl.BlockSpec((tq, 1), cur(0)),
            pl.BlockSpec((WINDOW, 1), prev(0)),
            pl.BlockSpec((1, LANES), lambda b, i: (0, 0)),
            pl.BlockSpec((1, LANES), lambda b, i: (0, 0)),
            pl.BlockSpec(memory_space=pltpu.SMEM),
            pl.BlockSpec((ATTN_WIDTH, D_MODEL), lambda b, i: (0, 0)),
        ],
        out_specs=pl.BlockSpec((tq, D_MODEL), cur(0)),
        out_shape=jax.ShapeDtypeStruct((T, D_MODEL), F32),
        compiler_params=_cparams("arbitrary", "arbitrary"),
        name="attention",
    )(proj, proj, proj, proj, proj, pos_col, pos_col, invf, sgn, sinks, w_o)


def _split3(a):
    hi = a.astype(BF16)
    r1 = a - hi.astype(F32)
    mid = r1.astype(BF16)
    lo = (r1 - mid.astype(F32)).astype(BF16)
    return hi, mid, lo


def _sel_dot(a, e):
    hi, mid, lo = _split3(a)
    d = lambda u: jnp.dot(u, e, preferred_element_type=F32)
    return d(hi) + d(mid) + d(lo)


def _ssd_body(z_ref, xs_ref, bm_ref, cm_ref, dt_ref, cwx_ref, cwb_ref, cwc_ref, cbx_ref, cbb_ref, cbc_ref,
              dtb_ref, alog_ref, dsk_ref, nw_ref, e64_ref, e128_ref, tril_ref, o_ref,
              state, tail_x, tail_b, tail_c):
    L = SSD_CHUNK

    @pl.when(pl.program_id(1) == 0)
    def _():
        state[...] = jnp.zeros_like(state)
        tail_x[...] = jnp.zeros_like(tail_x)
        tail_b[...] = jnp.zeros_like(tail_b)
        tail_c[...] = jnp.zeros_like(tail_c)

    row8 = lax.broadcasted_iota(I32, (8, 1), 0)

    def conv_silu(u, tail_ref, w_ref, b_ref):
        tail = tail_ref[...]
        acc = u * w_ref[SSD_CONV - 1:SSD_CONV, :] + b_ref[...]
        for j in range(1, SSD_CONV):
            ru = pltpu.roll(u, j, 0)
            head = jnp.where(row8 < j, pltpu.roll(tail, j, 0), ru[0:8])
            sh = jnp.concatenate([head, ru[8:]], axis=0)
            acc = acc + sh * w_ref[SSD_CONV - 1 - j:SSD_CONV - j, :]
        tail_ref[...] = u[L - 8:L]
        return acc * jax.nn.sigmoid(acc)

    xs = conv_silu(xs_ref[...], tail_x, cwx_ref, cbx_ref)
    bm = conv_silu(bm_ref[...], tail_b, cwb_ref, cbb_ref)
    cm = conv_silu(cm_ref[...], tail_c, cwc_ref, cbc_ref)

    dt = jax.nn.softplus(dt_ref[...] + dtb_ref[...])
    dA = dt * (-jnp.exp(alog_ref[...]))
    hi, mid, lo = _split3(dA)
    tril = tril_ref[...]
    cs = lambda u: jnp.dot(tril, u, preferred_element_type=F32)
    acum = cs(hi) + cs(mid) + cs(lo)
    acum_t = acum.T
    e64 = e64_ref[...]
    dt_x = _sel_dot(dt, e64)
    acum_x = _sel_dot(acum, e64)
    tot_x = acum_x[L - 1:L, :]
    xdt = xs * dt_x
    xdt_b = xdt.astype(BF16)
    wx_b = (jnp.exp(tot_x - acum_x) * xdt).astype(BF16)
    eac_x = jnp.exp(acum_x)
    etot_x = jnp.exp(tot_x)
    cm_b = cm.astype(BF16)
    bm_b = bm.astype(BF16)

    ti = lax.broadcasted_iota(I32, (L, L), 0)
    si = lax.broadcasted_iota(I32, (L, L), 1)
    causal = ti >= si
    lane = lax.broadcasted_iota(I32, (1, LANES), 1)
    mlo = lane < SSD_HEAD_DIM
    GW = SSD_WIDTH // SSD_GROUPS
    ys = []
    for g in range(SSD_GROUPS):
        cmg = cm_b[:, g * SSD_STATE:(g + 1) * SSD_STATE]
        bmg = bm_b[:, g * SSD_STATE:(g + 1) * SSD_STATE]
        cb = lax.dot_general(cmg, bmg, (((1,), (1,)), ((), ())), preferred_element_type=F32)
        colx = _sel_dot(acum, e128_ref[:, g * 8 * LANES:(g + 1) * 8 * LANES])
        tiles = []
        for j in range(4):
            xt = xdt_b[:, g * GW + j * LANES:g * GW + (j + 1) * LANES]
            acc = None
            for half in range(2):
                r = 2 * j + half
                h = 8 * g + r
                seg = colx[:, r * LANES:(r + 1) * LANES] - acum_t[h:h + 1, :]
                dec = jnp.exp(jnp.where(causal, seg, -jnp.inf))
                mm = (cb * dec).astype(BF16)
                xm = jnp.where(mlo, xt, 0.0) if half == 0 else jnp.where(mlo, 0.0, xt)
                y = jnp.dot(mm, xm.astype(BF16), preferred_element_type=F32)
                acc = y if acc is None else acc + y
            tiles.append(acc)
        y_intra = jnp.concatenate(tiles, axis=1)
        st = state[:, g * GW:(g + 1) * GW]
        y_inter = jnp.dot(cmg, st.astype(BF16), preferred_element_type=F32) * eac_x[:, g * GW:(g + 1) * GW]
        ys.append(y_intra + y_inter)
        bmt = bm[:, g * SSD_STATE:(g + 1) * SSD_STATE].T.astype(BF16)
        state[:, g * GW:(g + 1) * GW] = st * etot_x[:, g * GW:(g + 1) * GW] + jnp.dot(
            bmt, wx_b[:, g * GW:(g + 1) * GW], preferred_element_type=F32)

    y = jnp.concatenate(ys, axis=1) + dsk_ref[...] * xs
    z = z_ref[...]
    y = y * (z * jax.nn.sigmoid(z))
    outs = []
    for g in range(SSD_GROUPS):
        outs.append(_rms(y[:, g * GW:(g + 1) * GW], nw_ref[:, g * GW:(g + 1) * GW]))
    o_ref[...] = jnp.concatenate(outs, axis=1).astype(BF16)


def _ssd(proj, consts, B, S):
    T = B * S
    L = SSD_CHUNK
    nc = S // L
    blk = lambda w, col: pl.BlockSpec((L, w), lambda b, c: (b * nc + c, col))
    full = lambda a: pl.BlockSpec(a.shape, lambda b, c: (0,) * a.ndim)
    return pl.pallas_call(
        _ssd_body,
        grid=(B, nc),
        in_specs=[
            blk(SSD_WIDTH, COL_Z // SSD_WIDTH),
            blk(SSD_WIDTH, COL_XS // SSD_WIDTH),
            blk(BC_WIDTH, COL_BM // BC_WIDTH),
            blk(BC_WIDTH, COL_CM // BC_WIDTH),
            blk(LANES, COL_DT // LANES),
        ] + [full(a) for a in consts],
        out_specs=pl.BlockSpec((L, SSD_WIDTH), lambda b, c: (b * nc + c, 0)),
        out_shape=jax.ShapeDtypeStruct((T, SSD_WIDTH), BF16),
        scratch_shapes=[
            pltpu.VMEM((SSD_STATE, SSD_WIDTH), F32),
            pltpu.VMEM((8, SSD_WIDTH), F32),
            pltpu.VMEM((8, BC_WIDTH), F32),
            pltpu.VMEM((8, BC_WIDTH), F32),
        ],
        compiler_params=_cparams("arbitrary", "arbitrary"),
        name="ssd",
    )(proj, proj, proj, proj, proj, *consts)


def _merge_body(x_ref, at_ref, yn_ref, ga_ref, gs_ref, wso_ref, wout_ref, o_ref):
    ssd = jnp.dot(yn_ref[...], wso_ref[...], preferred_element_type=F32)
    m = jax.nn.sigmoid(ga_ref[...]) * at_ref[...] + jax.nn.sigmoid(gs_ref[...]) * ssd
    o_ref[...] = x_ref[...] + jnp.dot(m.astype(BF16), wout_ref[...], preferred_element_type=F32)


def _merge(xf, attn_o, yn, proj, w_ssd_o, w_out, tm=512):
    T = xf.shape[0]
    row = lambda w, col: pl.BlockSpec((tm, w), lambda m: (m, col))
    return pl.pallas_call(
        _merge_body,
        grid=(T // tm,),
        in_specs=[
            row(D_MODEL, 0), row(D_MODEL, 0), row(SSD_WIDTH, 0),
            row(D_MODEL, COL_GA // D_MODEL), row(D_MODEL, COL_GS // D_MODEL),
            pl.BlockSpec((SSD_WIDTH, D_MODEL), lambda m: (0, 0)),
            pl.BlockSpec((D_MODEL, D_MODEL), lambda m: (0, 0)),
        ],
        out_specs=row(D_MODEL, 0),
        out_shape=jax.ShapeDtypeStruct((T, D_MODEL), F32),
        compiler_params=_cparams("arbitrary"),
        name="merge",
    )(xf, attn_o, yn, proj, proj, w_ssd_o, w_out)


def _top16(s, payload=None):
    n = s.shape[0]
    iota = lax.broadcasted_iota(I32, s.shape, 0)
    vals, sel = [], []
    for _ in range(PEER_TOPK):
        m = jnp.max(s, axis=0, keepdims=True)
        am = jnp.min(jnp.where(s == m, iota, n), axis=0, keepdims=True)
        hit = iota == am
        vals.append(m)
        sel.append(am if payload is None else jnp.max(jnp.where(hit, payload, -1), axis=0, keepdims=True))
        s = jnp.where(hit, -jnp.inf, s)
    return jnp.concatenate(vals, axis=0), jnp.concatenate(sel, axis=0)


def _route_body(x_ref, nw_ref, wqt_ref, keys_ref, idx_ref, g_ref, qt_scr, it_scr, gt_scr):
    hn = _rms(x_ref[...], nw_ref[...]).astype(BF16)
    qt = lax.dot_general(wqt_ref[...], hn, (((1,), (1,)), ((), ())), preferred_element_type=F32)
    qt_scr[...] = qt.astype(BF16)

    def head(h, carry):
        r1 = pl.multiple_of(h * 2 * PEER_HALF, PEER_HALF)
        r2 = pl.multiple_of(h * 2 * PEER_HALF + PEER_HALF, PEER_HALF)
        s1 = jnp.dot(keys_ref[2 * h], qt_scr[pl.ds(r1, PEER_HALF), :], preferred_element_type=F32)
        s2 = jnp.dot(keys_ref[2 * h + 1], qt_scr[pl.ds(r2, PEER_HALF), :], preferred_element_type=F32)
        v1, i1 = _top16(s1)
        v2, i2 = _top16(s2)
        cands, ecands = [], []
        for i in range(PEER_TOPK):
            nj = PEER_TOPK // (i + 1)
            njp = 8 * ((nj + 7) // 8)
            cv = v1[i:i + 1] + v2[0:njp]
            if njp > nj:
                cv = jnp.where(lax.broadcasted_iota(I32, cv.shape, 0) < nj, cv, -jnp.inf)
            cands.append(cv)
            ecands.append(i1[i:i + 1] * PEER_N_KEYS + i2[0:njp])
        sc, e = _top16(jnp.concatenate(cands, axis=0), jnp.concatenate(ecands, axis=0))
        p = jnp.exp(sc - jnp.max(sc, axis=0, keepdims=True))
        gate = p / jnp.sum(p, axis=0, keepdims=True)
        o = pl.multiple_of(h * PEER_TOPK, PEER_TOPK)
        it_scr[pl.ds(o, PEER_TOPK), :] = e.astype(F32)
        gt_scr[pl.ds(o, PEER_TOPK), :] = gate
        return carry

    lax.fori_loop(0, PEER_HEADS, head, 0)
    idx_ref[...] = it_scr[...].T.astype(I32)
    g_ref[...] = gt_scr[...].T


def _route(x1, norm_w, wq_t, keys, tb=128):
    T = x1.shape[0]
    return pl.pallas_call(
        _route_body,
        grid=(T // tb,),
        in_specs=[
            pl.BlockSpec((tb, D_MODEL), lambda m: (m, 0)),
            pl.BlockSpec((1, D_MODEL), lambda m: (0, 0)),
            pl.BlockSpec(wq_t.shape, lambda m: (0, 0)),
            pl.BlockSpec(keys.shape, lambda m: (0, 0, 0)),
        ],
        out_specs=[pl.BlockSpec((tb, PEER_HK), lambda m: (m, 0)), pl.BlockSpec((tb, PEER_HK), lambda m: (m, 0))],
        out_shape=[jax.ShapeDtypeStruct((T, PEER_HK), I32), jax.ShapeDtypeStruct((T, PEER_HK), F32)],
        scratch_shapes=[
            pltpu.VMEM((wq_t.shape[0], tb), BF16),
            pltpu.VMEM((PEER_HK, tb), F32),
            pltpu.VMEM((PEER_HK, tb), F32),
        ],
        compiler_params=_cparams("arbitrary"),
        name="route",
    )(x1, norm_w, wq_t, keys)


NCH = D_MODEL // LANES
ROWGRP = PEER_HK // 8


def _rms3(x, w):
    ms = jnp.sum(jnp.sum(x * x, axis=2, keepdims=True), axis=1, keepdims=True) * (1.0 / D_MODEL)
    return x * lax.rsqrt(ms + EPS) * w


def _peer_body(idx0_ref, idxn_ref, x_ref, g_ref, nfw_ref, nlw_ref, uv_ref, o_ref, gbuf0, gbuf1, sem, hn_scr, acc,
               *, tb):
    i = pl.program_id(0)
    n = pl.num_programs(0)

    def issue_token(idx_ref, buf, s, t):
        for k in range(PEER_HK):
            pltpu.make_async_copy(uv_ref.at[idx_ref[t, k]], buf.at[t * ROWGRP + k // 8, :, k % 8, :],
                                  s).start(priority=k % 2)

    def wait_buf(buf, s):
        pltpu.make_async_copy(buf, buf, s).wait()

    @pl.when(i == 0)
    def _():
        def body(t, c):
            issue_token(idx0_ref, gbuf0, sem.at[0], t)
            return c
        lax.fori_loop(0, tb, body, 0)

    shape = (ROWGRP, 8, LANES)
    eye = lax.broadcasted_iota(I32, shape, 2) == (lax.broadcasted_iota(I32, shape, 0) * 8
                                                  + lax.broadcasted_iota(I32, shape, 1))

    def step(cur, cur_sem, nxt, nxt_sem):
        wait_buf(cur, cur_sem)
        x = x_ref[...]
        hn_scr[...] = _rms3(x, nfw_ref[...])

        def per_token(t, carry):
            issue_token(idxn_ref, nxt, nxt_sem, t)
            r0 = pl.multiple_of(t * ROWGRP, ROWGRP)
            sa = None
            for c in range(NCH):
                p = cur[pl.ds(r0, ROWGRP), c] * hn_scr[t, pl.ds(c, 1), :]
                sa = p if sa is None else sa + p
            a = jnp.sum(sa, axis=-1, keepdims=True)
            gcol = jnp.sum(jnp.where(eye, g_ref[pl.ds(t, 1), :], 0.0), axis=-1, keepdims=True)
            cc = gcol * (0.5 * a * (1.0 + lax.erf(a * (2.0 ** -0.5))))
            for c in range(NCH):
                oc = jnp.sum(cc * cur[pl.ds(r0, ROWGRP), NCH + c], axis=0)
                acc[t, pl.ds(c, 1), :] = jnp.sum(oc, axis=0, keepdims=True)
            return carry

        lax.fori_loop(0, tb, per_token, 0)
        o_ref[...] = _rms3(x + acc[...], nlw_ref[...])

        @pl.when(i == n - 1)
        def _():
            wait_buf(nxt, nxt_sem)

    @pl.when(i % 2 == 0)
    def _():
        step(gbuf0, sem.at[0], gbuf1, sem.at[1])

    @pl.when(i % 2 == 1)
    def _():
        step(gbuf1, sem.at[1], gbuf0, sem.at[0])


def _peer(idx, gates, x3, nfw, nlw, uv3, tb=16):
    T = x3.shape[0]
    nblk = T // tb
    tok = lambda m: (m, 0, 0)
    return pl.pallas_call(
        functools.partial(_peer_body, tb=tb),
        grid=(nblk,),
        in_specs=[
            pl.BlockSpec((tb, PEER_HK), lambda m: (m, 0), memory_space=pltpu.SMEM),
            pl.BlockSpec((tb, PEER_HK), lambda m: (jnp.minimum(m + 1, nblk - 1), 0), memory_space=pltpu.SMEM),
            pl.BlockSpec((tb, NCH, LANES), tok),
            pl.BlockSpec((tb, PEER_HK), lambda m: (m, 0)),
            pl.BlockSpec((1, NCH, LANES), lambda m: (0, 0, 0)),
            pl.BlockSpec((1, NCH, LANES), lambda m: (0, 0, 0)),
            pl.BlockSpec(memory_space=pl.ANY),
        ],
        out_specs=pl.BlockSpec((tb, NCH, LANES), tok),
        out_shape=jax.ShapeDtypeStruct((T, NCH, LANES), F32),
        scratch_shapes=[
            pltpu.VMEM((tb * ROWGRP, 2 * NCH, 8, LANES), F32),
            pltpu.VMEM((tb * ROWGRP, 2 * NCH, 8, LANES), F32),
            pltpu.SemaphoreType.DMA((2,)),
            pltpu.VMEM((tb, NCH, LANES), F32),
            pltpu.VMEM((tb, NCH, LANES), F32),
        ],
        compiler_params=_cparams("arbitrary"),
        name="peer",
    )(idx, idx, x3, gates, nfw, nlw, uv3)


def _layer_weights(w_in, conv_w, conv_b, dt_bias, a_log, d_skip):
    D = D_MODEL
    q, k, v, z, xbc, dt, gates = jnp.split(
        w_in, [ATTN_WIDTH, ATTN_WIDTH + KV_WIDTH, ATTN_WIDTH + 2 * KV_WIDTH,
               ATTN_WIDTH + 2 * KV_WIDTH + SSD_WIDTH,
               ATTN_WIDTH + 2 * KV_WIDTH + SSD_WIDTH + SSD_WIDTH + 2 * BC_WIDTH,
               ATTN_WIDTH + 2 * KV_WIDTH + SSD_WIDTH + SSD_WIDTH + 2 * BC_WIDTH + SSD_HEADS], axis=1)
    dup = lambda t: jnp.concatenate([t.reshape(D, N_KV_HEADS, HEAD_DIM)] * 2, axis=-1).reshape(D, 2 * KV_WIDTH)
    pad = PROJ_COLS - (COL_DT + SSD_HEADS)
    w_perm = jnp.concatenate([z, xbc, gates, q, dup(k), dup(v), dt, jnp.zeros((D, pad), w_in.dtype)],
                             axis=1).astype(BF16)
    lanes_pad = lambda a: jnp.pad(a.reshape(1, SSD_HEADS), ((0, 0), (0, LANES - SSD_HEADS)))
    ssd_consts = [
        conv_w[:, :SSD_WIDTH], conv_w[:, SSD_WIDTH:SSD_WIDTH + BC_WIDTH], conv_w[:, SSD_WIDTH + BC_WIDTH:],
        conv_b[None, :SSD_WIDTH], conv_b[None, SSD_WIDTH:SSD_WIDTH + BC_WIDTH], conv_b[None, SSD_WIDTH + BC_WIDTH:],
        lanes_pad(dt_bias), lanes_pad(a_log),
        jnp.repeat(d_skip, SSD_HEAD_DIM)[None, :],
    ]
    return w_perm, ssd_consts


def _constants():
    lane = jnp.arange(LANES)
    hl = lane % HEAD_DIM
    half = ROT_DIM // 2
    freq = ROPE_THETA ** (-jnp.arange(0, ROT_DIM, 2, dtype=F32) / ROT_DIM)
    invf = jnp.where(hl < ROT_DIM, freq[hl % half], 0.0).astype(F32)[None, :]
    sgn = jnp.where(hl < half, -1.0, jnp.where(hl < ROT_DIM, 1.0, 0.0)).astype(F32)[None, :]
    hrow = jnp.arange(LANES)[:, None]
    e64 = (hrow == (jnp.arange(SSD_WIDTH)[None, :] // SSD_HEAD_DIM)).astype(BF16)
    e128 = (hrow == (jnp.arange(SSD_HEADS * LANES)[None, :] // LANES)).astype(BF16)
    tril = (jnp.arange(SSD_CHUNK)[:, None] >= jnp.arange(SSD_CHUNK)[None, :]).astype(BF16)
    return invf, sgn, e64, e128, tril


def kernel(x, positions, norm_mix_w, w_in, attn_sinks, conv_w, conv_b, dt_bias, a_log, d_skip, ssd_norm_w,
           w_attn_o, w_ssd_o, w_out, norm_ffn_w, peer_wq, peer_keys, peer_u, peer_v, norm_final_w):
    B, S, D = x.shape
    T = B * S
    assert w_in.shape[0] == 1, "single-layer block: the final norm is fused into the PEER kernel"
    l = 0
    invf, sgn, e64, e128, tril = _constants()
    pos_col = positions.reshape(T, 1)
    xf = x.reshape(T, D)
    w_perm, ssd_consts = _layer_weights(w_in[l], conv_w[l], conv_b[l], dt_bias[l], a_log[l], d_skip[l])
    proj = _inproj(xf, norm_mix_w[l][None, :], w_perm)
    attn_o = _attention(proj, pos_col, invf, sgn, attn_sinks[l], w_attn_o[l].astype(BF16), B, S)
    yn = _ssd(proj, ssd_consts + [ssd_norm_w[l][None, :], e64, e128, tril], B, S)
    x1 = _merge(xf, attn_o, yn, proj, w_ssd_o[l].astype(BF16), w_out[l].astype(BF16))
    keys = peer_keys[l].reshape(PEER_HEADS * 2, PEER_N_KEYS, PEER_HALF).astype(BF16)
    idx, gates = _route(x1, norm_ffn_w[l][None, :], peer_wq[l].T.astype(BF16), keys)
    uv3 = jnp.concatenate([peer_u[l].reshape(-1, NCH, LANES), peer_v[l].reshape(-1, NCH, LANES)], axis=1)
    r3 = lambda a: a.reshape(1, NCH, LANES)
    out = _peer(idx, gates, x1.reshape(T, NCH, LANES), r3(norm_ffn_w[l]), r3(norm_final_w), uv3)
    return out.reshape(B, S, D)
```

```python
import functools
import math

import jax
import jax.numpy as jnp
from jax import lax
from jax.experimental import pallas as pl
from jax.experimental.pallas import tpu as pltpu

F32 = jnp.float32
BF16 = jnp.bfloat16
I32 = jnp.int32

D_MODEL = 1024
N_Q_HEADS = 16
N_KV_HEADS = 4
HEAD_DIM = 64
ATTN_WIDTH = N_Q_HEADS * HEAD_DIM
KV_WIDTH = N_KV_HEADS * HEAD_DIM
WINDOW = 128
ROT_DIM = HEAD_DIM // 4
ROPE_THETA = 500000.0
SSD_WIDTH = 2 * D_MODEL
SSD_HEAD_DIM = 64
SSD_HEADS = SSD_WIDTH // SSD_HEAD_DIM
SSD_GROUPS = 4
SSD_STATE = 128
SSD_CONV = 4
SSD_CHUNK = 128
BC_WIDTH = SSD_GROUPS * SSD_STATE
PEER_HEADS = 8
PEER_N_KEYS = 128
PEER_TOPK = 16
PEER_HALF = 128
PEER_HK = PEER_HEADS * PEER_TOPK
EPS = 1e-6

LANES = 128
VMEM_LIMIT = 56 * 1024 * 1024

COL_Z = 0
COL_XS = COL_Z + SSD_WIDTH
COL_BM = COL_XS + SSD_WIDTH
COL_CM = COL_BM + BC_WIDTH
COL_GA = COL_CM + BC_WIDTH
COL_GS = COL_GA + D_MODEL
COL_Q = COL_GS + D_MODEL
COL_K = COL_Q + ATTN_WIDTH
COL_V = COL_K + 2 * KV_WIDTH
COL_DT = COL_V + 2 * KV_WIDTH
PROJ_TN = 2432
PROJ_COLS = 4 * PROJ_TN
assert COL_DT + LANES <= PROJ_COLS


def _cparams(*sem):
    return pltpu.CompilerParams(dimension_semantics=sem, vmem_limit_bytes=VMEM_LIMIT)


def _rms(x, w):
    return x * lax.rsqrt(jnp.mean(x * x, axis=-1, keepdims=True) + EPS) * w


def _inproj_body(x_ref, nw_ref, w_ref, o_ref):
    h = _rms(x_ref[...], nw_ref[...])
    o_ref[...] = jnp.dot(h.astype(BF16), w_ref[...], preferred_element_type=F32)


def _inproj(xf, norm_w, w_perm, tm=512):
    T = xf.shape[0]
    return pl.pallas_call(
        _inproj_body,
        grid=(PROJ_COLS // PROJ_TN, T // tm),
        in_specs=[
            pl.BlockSpec((tm, D_MODEL), lambda n, m: (m, 0)),
            pl.BlockSpec((1, D_MODEL), lambda n, m: (0, 0)),
            pl.BlockSpec((D_MODEL, PROJ_TN), lambda n, m: (0, n)),
        ],
        out_specs=pl.BlockSpec((tm, PROJ_TN), lambda n, m: (m, n)),
        out_shape=jax.ShapeDtypeStruct((T, PROJ_COLS), F32),
        compiler_params=_cparams("arbitrary", "arbitrary"),
        name="inproj",
    )(xf, norm_w, w_perm)


def _attn_body(q_ref, kc_ref, vc_ref, kp_ref, vp_ref, pos_ref, posp_ref, invf_ref, sgn_ref,
               sink_ref, wo_ref, o_ref, *, tq):
    i = pl.program_id(1)
    lane = lax.broadcasted_iota(I32, (1, LANES), 1)
    lo8 = (lane % HEAD_DIM) < (ROT_DIM // 2)
    mlo = lane < HEAD_DIM

    def rope(t, pos):
        ang = pos.astype(F32) * invf_ref[...]
        cs = jnp.cos(ang)
        sn = jnp.sin(ang) * sgn_ref[...]
        outs = []
        for j in range(t.shape[1] // LANES):
            tj = t[:, j * LANES:(j + 1) * LANES]
            sh = jnp.where(lo8, pltpu.roll(tj, LANES - ROT_DIM // 2, 1), pltpu.roll(tj, ROT_DIM // 2, 1))
            outs.append(tj * cs + sh * sn)
        return jnp.concatenate(outs, axis=1)

    pos = pos_ref[...]
    qrot = rope(q_ref[...], pos)
    kfull = jnp.concatenate([rope(kp_ref[...], posp_ref[...]), rope(kc_ref[...], pos)], axis=0).astype(BF16)
    vfull = jnp.concatenate([vp_ref[...], vc_ref[...]], axis=0).astype(BF16)

    qi = lax.broadcasted_iota(I32, (WINDOW, 2 * WINDOW), 0)
    kj = lax.broadcasted_iota(I32, (WINDOW, 2 * WINDOW), 1)
    rel = WINDOW + qi - kj
    band = (rel >= 0) & (rel < WINDOW)
    rgrp = lax.broadcasted_iota(I32, (4 * WINDOW, 1), 0) // WINDOW
    scale = HEAD_DIM ** -0.5

    for c in range(tq // WINDOW):
        valid = band
        if c == 0:
            valid = band & ((kj >= WINDOW) | (i > 0))
        valid4 = jnp.concatenate([valid] * 4, axis=0)
        tiles = []
        for h in range(N_KV_HEADS):
            kh = kfull[c * WINDOW:c * WINDOW + 2 * WINDOW, h * LANES:(h + 1) * LANES]
            vh = vfull[c * WINDOW:c * WINDOW + 2 * WINDOW, h * LANES:(h + 1) * LANES]
            t0 = qrot[c * WINDOW:(c + 1) * WINDOW, (2 * h) * LANES:(2 * h + 1) * LANES]
            t1 = qrot[c * WINDOW:(c + 1) * WINDOW, (2 * h + 1) * LANES:(2 * h + 2) * LANES]
            qg = jnp.concatenate([jnp.where(mlo, t0, 0.0), jnp.where(mlo, 0.0, t0),
                                  jnp.where(mlo, t1, 0.0), jnp.where(mlo, 0.0, t1)], axis=0).astype(BF16)
            s = lax.dot_general(qg, kh, (((1,), (1,)), ((), ())), preferred_element_type=F32) * scale
            s = jnp.where(valid4, s, -jnp.inf)
            sk = jnp.where(rgrp == 0, sink_ref[4 * h],
                           jnp.where(rgrp == 1, sink_ref[4 * h + 1],
                                     jnp.where(rgrp == 2, sink_ref[4 * h + 2], sink_ref[4 * h + 3])))
            m = jnp.maximum(jnp.max(s, axis=-1, keepdims=True), sk)
            p = jnp.exp(s - m)
            den = jnp.sum(p, axis=-1, keepdims=True) + jnp.exp(sk - m)
            o = jnp.dot(p.astype(BF16), vh, preferred_element_type=F32) / den
            tiles.append(jnp.where(mlo, o[0:WINDOW], o[WINDOW:2 * WINDOW]))
            tiles.append(jnp.where(mlo, o[2 * WINDOW:3 * WINDOW], o[3 * WINDOW:4 * WINDOW]))
        attn = jnp.concatenate(tiles, axis=1).astype(BF16)
        o_ref[c * WINDOW:(c + 1) * WINDOW, :] = jnp.dot(attn, wo_ref[...], preferred_element_type=F32)


def _attention(proj, pos_col, invf, sgn, sinks, w_o, B, S, tq=512):
    T = B * S
    nq = S // tq
    nb = S // WINDOW
    r = tq // WINDOW
    cur = lambda col: (lambda b, i: (b * nq + i, col))
    prev = lambda col: (lambda b, i: (b * nb + jnp.maximum(i * r - 1, 0), col))
    return pl.pallas_call(
        functools.partial(_attn_body, tq=tq),
        grid=(B, nq),
        in_specs=[
            pl.BlockSpec((tq, ATTN_WIDTH), cur(COL_Q // ATTN_WIDTH)),
            pl.BlockSpec((tq, 2 * KV_WIDTH), cur(COL_K // (2 * KV_WIDTH))),
            pl.BlockSpec((tq, 2 * KV_WIDTH), cur(COL_V // (2 * KV_WIDTH))),
            pl.BlockSpec((WINDOW, 2 * KV_WIDTH), prev(COL_K // (2 * KV_WIDTH))),
            pl.BlockSpec((WINDOW, 2 * KV_WIDTH), prev(COL_V // (2 * KV_WIDTH))),
            pl.BlockSpec((tq, 1), cur(0)),
            pl.BlockSpec((WINDOW, 1), prev(0)),
            pl.BlockSpec((1, LANES), lambda b, i: (0, 0)),
            pl.BlockSpec((1, LANES), lambda b, i: (0, 0)),
            pl.BlockSpec(memory_space=pltpu.SMEM),
            pl.BlockSpec((ATTN_WIDTH, D_MODEL), lambda b, i: (0, 0)),
        ],
        out_specs=pl.BlockSpec((tq, D_MODEL), cur(0)),
        out_shape=jax.ShapeDtypeStruct((T, D_MODEL), F32),
        compiler_params=_cparams("arbitrary", "arbitrary"),
        name="attention",
    )(proj, proj, proj, proj, proj, pos_col, pos_col, invf, sgn, sinks, w_o)


def _split3(a):
    hi = a.astype(BF16)
    r1 = a - hi.astype(F32)
    mid = r1.astype(BF16)
    lo = (r1 - mid.astype(F32)).astype(BF16)
    return hi, mid, lo


def _sel_dot(a, e):
    hi, mid, lo = _split3(a)
    d = lambda u: jnp.dot(u, e, preferred_element_type=F32)
    return d(hi) + d(mid) + d(lo)


def _ssd_body(z_ref, xs_ref, bm_ref, cm_ref, dt_ref, cwx_ref, cwb_ref, cwc_ref, cbx_ref, cbb_ref, cbc_ref,
              dtb_ref, alog_ref, dsk_ref, nw_ref, e64_ref, e128_ref, tril_ref, o_ref,
              state, tail_x, tail_b, tail_c):
    L = SSD_CHUNK

    @pl.when(pl.program_id(1) == 0)
    def _():
        state[...] = jnp.zeros_like(state)
        tail_x[...] = jnp.zeros_like(tail_x)
        tail_b[...] = jnp.zeros_like(tail_b)
        tail_c[...] = jnp.zeros_like(tail_c)

    row8 = lax.broadcasted_iota(I32, (8, 1), 0)

    def conv_silu(u, tail_ref, w_ref, b_ref):
        tail = tail_ref[...]
        acc = u * w_ref[SSD_CONV - 1:SSD_CONV, :] + b_ref[...]
        for j in range(1, SSD_CONV):
            ru = pltpu.roll(u, j, 0)
            head = jnp.where(row8 < j, pltpu.roll(tail, j, 0), ru[0:8])
            sh = jnp.concatenate([head, ru[8:]], axis=0)
            acc = acc + sh * w_ref[SSD_CONV - 1 - j:SSD_CONV - j, :]
        tail_ref[...] = u[L - 8:L]
        return acc * jax.nn.sigmoid(acc)

    xs = conv_silu(xs_ref[...], tail_x, cwx_ref, cbx_ref)
    bm = conv_silu(bm_ref[...], tail_b, cwb_ref, cbb_ref)
    cm = conv_silu(cm_ref[...], tail_c, cwc_ref, cbc_ref)

    dt = jax.nn.softplus(dt_ref[...] + dtb_ref[...])
    dA = dt * (-jnp.exp(alog_ref[...]))
    hi, mid, lo = _split3(dA)
    tril = tril_ref[...]
    cs = lambda u: jnp.dot(tril, u, preferred_element_type=F32)
    acum = cs(hi) + cs(mid) + cs(lo)
    acum_t = acum.T
    e64 = e64_ref[...]
    dt_x = _sel_dot(dt, e64)
    acum_x = _sel_dot(acum, e64)
    tot_x = acum_x[L - 1:L, :]
    xdt = xs * dt_x
    xdt_b = xdt.astype(BF16)
    wx_b = (jnp.exp(tot_x - acum_x) * xdt).astype(BF16)
    eac_x = jnp.exp(acum_x)
    etot_x = jnp.exp(tot_x)
    cm_b = cm.astype(BF16)
    bm_b = bm.astype(BF16)

    ti = lax.broadcasted_iota(I32, (L, L), 0)
    si = lax.broadcasted_iota(I32, (L, L), 1)
    causal = ti >= si
    lane = lax.broadcasted_iota(I32, (1, LANES), 1)
    mlo = lane < SSD_HEAD_DIM
    GW = SSD_WIDTH // SSD_GROUPS
    ys = []
    for g in range(SSD_GROUPS):
        cmg = cm_b[:, g * SSD_STATE:(g + 1) * SSD_STATE]
        bmg = bm_b[:, g * SSD_STATE:(g + 1) * SSD_STATE]
        cb = lax.dot_general(cmg, bmg, (((1,), (1,)), ((), ())), preferred_element_type=F32)
        colx = _sel_dot(acum, e128_ref[:, g * 8 * LANES:(g + 1) * 8 * LANES])
        tiles = []
        for j in range(4):
            xt = xdt_b[:, g * GW + j * LANES:g * GW + (j + 1) * LANES]
            acc = None
            for half in range(2):
                r = 2 * j + half
                h = 8 * g + r
                seg = colx[:, r * LANES:(r + 1) * LANES] - acum_t[h:h + 1, :]
                dec = jnp.exp(jnp.where(causal, seg, -jnp.inf))
                mm = (cb * dec).astype(BF16)
                xm = jnp.where(mlo, xt, 0.0) if half == 0 else jnp.where(mlo, 0.0, xt)
                y = jnp.dot(mm, xm.astype(BF16), preferred_element_type=F32)
                acc = y if acc is None else acc + y
            tiles.append(acc)
        y_intra = jnp.concatenate(tiles, axis=1)
        st = state[:, g * GW:(g + 1) * GW]
        y_inter = jnp.dot(cmg, st.astype(BF16), preferred_element_type=F32) * eac_x[:, g * GW:(g + 1) * GW]
        ys.append(y_intra + y_inter)
        bmt = bm[:, g * SSD_STATE:(g + 1) * SSD_STATE].T.astype(BF16)
        state[:, g * GW:(g + 1) * GW] = st * etot_x[:, g * GW:(g + 1) * GW] + jnp.dot(
            bmt, wx_b[:, g * GW:(g + 1) * GW], preferred_element_type=F32)

    y = jnp.concatenate(ys, axis=1) + dsk_ref[...] * xs
    z = z_ref[...]
    y = y * (z * jax.nn.sigmoid(z))
    outs = []
    for g in range(SSD_GROUPS):
        outs.append(_rms(y[:, g * GW:(g + 1) * GW], nw_ref[:, g * GW:(g + 1) * GW]))
    o_ref[...] = jnp.concatenate(outs, axis=1).astype(BF16)


def _ssd(proj, consts, B, S):
    T = B * S
    L = SSD_CHUNK
    nc = S // L
    blk = lambda w, col: pl.BlockSpec((L, w), lambda b, c: (b * nc + c, col))
    full = lambda a: pl.BlockSpec(a.shape, lambda b, c: (0,) * a.ndim)
    return pl.pallas_call(
        _ssd_body,
        grid=(B, nc),
        in_specs=[
            blk(SSD_WIDTH, COL_Z // SSD_WIDTH),
            blk(SSD_WIDTH, COL_XS // SSD_WIDTH),
            blk(BC_WIDTH, COL_BM // BC_WIDTH),
            blk(BC_WIDTH, COL_CM // BC_WIDTH),
            blk(LANES, COL_DT // LANES),
        ] + [full(a) for a in consts],
        out_specs=pl.BlockSpec((L, SSD_WIDTH), lambda b, c: (b * nc + c, 0)),
        out_shape=jax.ShapeDtypeStruct((T, SSD_WIDTH), BF16),
        scratch_shapes=[
            pltpu.VMEM((SSD_STATE, SSD_WIDTH), F32),
            pltpu.VMEM((8, SSD_WIDTH), F32),
            pltpu.VMEM((8, BC_WIDTH), F32),
            pltpu.VMEM((8, BC_WIDTH), F32),
        ],
        compiler_params=_cparams("arbitrary", "arbitrary"),
        name="ssd",
    )(proj, proj, proj, proj, proj, *consts)


def _merge_body(x_ref, at_ref, yn_ref, ga_ref, gs_ref, wso_ref, wout_ref, o_ref):
    ssd = jnp.dot(yn_ref[...], wso_ref[...], preferred_element_type=F32)
    m = jax.nn.sigmoid(ga_ref[...]) * at_ref[...] + jax.nn.sigmoid(gs_ref[...]) * ssd
    o_ref[...] = x_ref[...] + jnp.dot(m.astype(BF16), wout_ref[...], preferred_element_type=F32)


def _merge(xf, attn_o, yn, proj, w_ssd_o, w_out, tm=512):
    T = xf.shape[0]
    row = lambda w, col: pl.BlockSpec((tm, w), lambda m: (m, col))
    return pl.pallas_call(
        _merge_body,
        grid=(T // tm,),
        in_specs=[
            row(D_MODEL, 0), row(D_MODEL, 0), row(SSD_WIDTH, 0),
            row(D_MODEL, COL_GA // D_MODEL), row(D_MODEL, COL_GS // D_MODEL),
            pl.BlockSpec((SSD_WIDTH, D_MODEL), lambda m: (0, 0)),
            pl.BlockSpec((D_MODEL, D_MODEL), lambda m: (0, 0)),
        ],
        out_specs=row(D_MODEL, 0),
        out_shape=jax.ShapeDtypeStruct((T, D_MODEL), F32),
        compiler_params=_cparams("arbitrary"),
        name="merge",
    )(xf, attn_o, yn, proj, proj, w_ssd_o, w_out)


def _top16(ss, payloads=None):
    n = ss[0].shape[0]
    iota = lax.broadcasted_iota(I32, ss[0].shape, 0).astype(F32)
    ss = list(ss)
    vals = [[] for _ in ss]
    sel = [[] for _ in ss]
    for _ in range(PEER_TOPK):
        for c in range(len(ss)):
            s = ss[c]
            m = jnp.max(s, axis=0, keepdims=True)
            tied = jnp.where(s == m, iota, float(n))
            am = jnp.min(tied, axis=0, keepdims=True)
            hit = tied == am
            vals[c].append(m)
            sel[c].append(am if payloads is None
                          else jnp.max(jnp.where(hit, payloads[c], -1.0), axis=0, keepdims=True))
            ss[c] = jnp.where(hit, -jnp.inf, s)
    cat = lambda rows: jnp.concatenate(rows, axis=0)
    return [cat(v) for v in vals], [cat(v) for v in sel]


def _pair_candidates(v1, i1, v2, i2):
    cands, ecands = [], []
    for i in range(PEER_TOPK):
        nj = PEER_TOPK // (i + 1)
        njp = 8 * ((nj + 7) // 8)
        cv = v1[i:i + 1] + v2[0:njp]
        if njp > nj:
            cv = jnp.where(lax.broadcasted_iota(I32, cv.shape, 0) < nj, cv, -jnp.inf)
        cands.append(cv)
        ecands.append(i1[i:i + 1] * PEER_N_KEYS + i2[0:njp])
    return jnp.concatenate(cands, axis=0), jnp.concatenate(ecands, axis=0)


ROUTE_HEADS_PER_ITER = 4


def _route_body(x_ref, nw_ref, wqt_ref, keys_ref, idx_ref, g_ref, qt_scr, it_scr, gt_scr):
    hn = _rms(x_ref[...], nw_ref[...]).astype(BF16)
    qt = lax.dot_general(wqt_ref[...], hn, (((1,), (1,)), ((), ())), preferred_element_type=F32)
    qt_scr[...] = qt.astype(BF16)

    def heads(it, carry):
        scores = []
        for u in range(ROUTE_HEADS_PER_ITER):
            for half in range(2):
                kidx = (it * ROUTE_HEADS_PER_ITER + u) * 2 + half
                r = pl.multiple_of(kidx * PEER_HALF, PEER_HALF)
                scores.append(jnp.dot(keys_ref[kidx], qt_scr[pl.ds(r, PEER_HALF), :], preferred_element_type=F32))
        v, ix = _top16(scores)
        pairs = [_pair_candidates(v[2 * u], ix[2 * u], v[2 * u + 1], ix[2 * u + 1])
                 for u in range(ROUTE_HEADS_PER_ITER)]
        sc, e = _top16([p[0] for p in pairs], [p[1] for p in pairs])
        for u in range(ROUTE_HEADS_PER_ITER):
            p = jnp.exp(sc[u] - jnp.max(sc[u], axis=0, keepdims=True))
            gate = p / jnp.sum(p, axis=0, keepdims=True)
            o = pl.multiple_of((it * ROUTE_HEADS_PER_ITER + u) * PEER_TOPK, PEER_TOPK)
            it_scr[pl.ds(o, PEER_TOPK), :] = e[u]
            gt_scr[pl.ds(o, PEER_TOPK), :] = gate
        return carry

    lax.fori_loop(0, PEER_HEADS // ROUTE_HEADS_PER_ITER, heads, 0)
    idx_ref[...] = it_scr[...].T.astype(I32)
    g_ref[...] = gt_scr[...].T


def _route(x1, norm_w, wq_t, keys, tb=128):
    T = x1.shape[0]
    return pl.pallas_call(
        _route_body,
        grid=(T // tb,),
        in_specs=[
            pl.BlockSpec((tb, D_MODEL), lambda m: (m, 0)),
            pl.BlockSpec((1, D_MODEL), lambda m: (0, 0)),
            pl.BlockSpec(wq_t.shape, lambda m: (0, 0)),
            pl.BlockSpec(keys.shape, lambda m: (0, 0, 0)),
        ],
        out_specs=[pl.BlockSpec((tb, PEER_HK), lambda m: (m, 0)), pl.BlockSpec((tb, PEER_HK), lambda m: (m, 0))],
        out_shape=[jax.ShapeDtypeStruct((T, PEER_HK), I32), jax.ShapeDtypeStruct((T, PEER_HK), F32)],
        scratch_shapes=[
            pltpu.VMEM((wq_t.shape[0], tb), BF16),
            pltpu.VMEM((PEER_HK, tb), F32),
            pltpu.VMEM((PEER_HK, tb), F32),
        ],
        compiler_params=_cparams("arbitrary"),
        name="route",
    )(x1, norm_w, wq_t, keys)


NCH = D_MODEL // LANES
ROWGRP = PEER_HK // 8


def _rms3(x, w):
    ms = jnp.sum(jnp.sum(x * x, axis=2, keepdims=True), axis=1, keepdims=True) * (1.0 / D_MODEL)
    return x * lax.rsqrt(ms + EPS) * w


def _sublane_totals(p):
    sub = lax.broadcasted_iota(I32, p[0].shape, 1)

    def comb(a, b, h, phase):
        m = ((sub - phase) & (2 * h - 1)) < h
        return jnp.where(m, a, b) + pltpu.roll(jnp.where(m, b, a), h, 1)

    c = comb(p[0], p[4], 4, 5)
    d = comb(p[2], p[6], 4, 7)
    e = comb(p[1], p[5], 4, 6)
    f = comb(p[3], p[7], 4, 0)
    return comb(comb(c, d, 2, 3), comb(e, f, 2, 0), 1, 0)


def _peer_body(idx0_ref, idxn_ref, x_ref, g_ref, nfw_ref, nlw_ref, uv_ref, o_ref, gbuf0, gbuf1, sem, hn_scr, cs_scr,
               acc, *, tb):
    i = pl.program_id(0)
    n = pl.num_programs(0)

    def issue_token(idx_ref, buf, s, t):
        for k in range(PEER_HK):
            pltpu.make_async_copy(uv_ref.at[idx_ref[t, k]], buf.at[t * PEER_HK + k], s).start(priority=k % 2)

    def wait_buf(buf, s):
        pltpu.make_async_copy(buf, buf, s).wait()

    @pl.when(i == 0)
    def _():
        def body(t, c):
            issue_token(idx0_ref, gbuf0, sem.at[0], t)
            return c
        lax.fori_loop(0, tb, body, 0)

    shape = (ROWGRP, 8, LANES)
    eye = lax.broadcasted_iota(I32, shape, 2) == (lax.broadcasted_iota(I32, shape, 1) * ROWGRP
                                                  + lax.broadcasted_iota(I32, shape, 0))

    def step(cur, cur_sem, nxt, nxt_sem):
        wait_buf(cur, cur_sem)
        x = x_ref[...]
        hn_scr[...] = _rms3(x, nfw_ref[...])

        def per_token(t, carry):
            issue_token(idxn_ref, nxt, nxt_sem, t)
            base = pl.multiple_of(t * PEER_HK, PEER_HK)
            hn = hn_scr[t]
            prods = [cur[pl.ds(base + j * ROWGRP, ROWGRP), 0:NCH, :] * hn for j in range(8)]
            a = jnp.sum(_sublane_totals(prods), axis=-1, keepdims=True)
            gcol = jnp.sum(jnp.where(eye, g_ref[pl.ds(t, 1), :], 0.0), axis=-1, keepdims=True)
            cc = gcol * (0.5 * a * (1.0 + lax.erf(a * (2.0 ** -0.5))))
            cs_scr[...] = jnp.broadcast_to(cc, shape)
            parts = [None] * 4
            for k in range(PEER_HK):
                j, g = divmod(k, ROWGRP)
                term = cs_scr[g, pl.ds(j, 1), :] * cur[base + k, NCH:2 * NCH, :]
                parts[k % 4] = term if parts[k % 4] is None else parts[k % 4] + term
            acc[t] = (parts[0] + parts[1]) + (parts[2] + parts[3])
            return carry

        lax.fori_loop(0, tb, per_token, 0)
        o_ref[...] = _rms3(x + acc[...], nlw_ref[...])

        @pl.when(i == n - 1)
        def _():
            wait_buf(nxt, nxt_sem)

    @pl.when(i % 2 == 0)
    def _():
        step(gbuf0, sem.at[0], gbuf1, sem.at[1])

    @pl.when(i % 2 == 1)
    def _():
        step(gbuf1, sem.at[1], gbuf0, sem.at[0])


def _peer(idx, gates, x3, nfw, nlw, uv3, tb=16):
    T = x3.shape[0]
    nblk = T // tb
    tok = lambda m: (m, 0, 0)
    return pl.pallas_call(
        functools.partial(_peer_body, tb=tb),
        grid=(nblk,),
        in_specs=[
            pl.BlockSpec((tb, PEER_HK), lambda m: (m, 0), memory_space=pltpu.SMEM),
            pl.BlockSpec((tb, PEER_HK), lambda m: (jnp.minimum(m + 1, nblk - 1), 0), memory_space=pltpu.SMEM),
            pl.BlockSpec((tb, NCH, LANES), tok),
            pl.BlockSpec((tb, PEER_HK), lambda m: (m, 0)),
            pl.BlockSpec((1, NCH, LANES), lambda m: (0, 0, 0)),
            pl.BlockSpec((1, NCH, LANES), lambda m: (0, 0, 0)),
            pl.BlockSpec(memory_space=pl.ANY),
        ],
        out_specs=pl.BlockSpec((tb, NCH, LANES), tok),
        out_shape=jax.ShapeDtypeStruct((T, NCH, LANES), F32),
        scratch_shapes=[
            pltpu.VMEM((tb * PEER_HK, 2 * NCH, LANES), F32),
            pltpu.VMEM((tb * PEER_HK, 2 * NCH, LANES), F32),
            pltpu.SemaphoreType.DMA((2,)),
            pltpu.VMEM((tb, NCH, LANES), F32),
            pltpu.VMEM((ROWGRP, 8, LANES), F32),
            pltpu.VMEM((tb, NCH, LANES), F32),
        ],
        compiler_params=_cparams("arbitrary"),
        name="peer",
    )(idx, idx, x3, gates, nfw, nlw, uv3)


def _layer_weights(w_in, conv_w, conv_b, dt_bias, a_log, d_skip):
    D = D_MODEL
    q, k, v, z, xbc, dt, gates = jnp.split(
        w_in, [ATTN_WIDTH, ATTN_WIDTH + KV_WIDTH, ATTN_WIDTH + 2 * KV_WIDTH,
               ATTN_WIDTH + 2 * KV_WIDTH + SSD_WIDTH,
               ATTN_WIDTH + 2 * KV_WIDTH + SSD_WIDTH + SSD_WIDTH + 2 * BC_WIDTH,
               ATTN_WIDTH + 2 * KV_WIDTH + SSD_WIDTH + SSD_WIDTH + 2 * BC_WIDTH + SSD_HEADS], axis=1)
    dup = lambda t: jnp.concatenate([t.reshape(D, N_KV_HEADS, HEAD_DIM)] * 2, axis=-1).reshape(D, 2 * KV_WIDTH)
    pad = PROJ_COLS - (COL_DT + SSD_HEADS)
    w_perm = jnp.concatenate([z, xbc, gates, q, dup(k), dup(v), dt, jnp.zeros((D, pad), w_in.dtype)],
                             axis=1).astype(BF16)
    lanes_pad = lambda a: jnp.pad(a.reshape(1, SSD_HEADS), ((0, 0), (0, LANES - SSD_HEADS)))
    ssd_consts = [
        conv_w[:, :SSD_WIDTH], conv_w[:, SSD_WIDTH:SSD_WIDTH + BC_WIDTH], conv_w[:, SSD_WIDTH + BC_WIDTH:],
        conv_b[None, :SSD_WIDTH], conv_b[None, SSD_WIDTH:SSD_WIDTH + BC_WIDTH], conv_b[None, SSD_WIDTH + BC_WIDTH:],
        lanes_pad(dt_bias), lanes_pad(a_log),
        jnp.repeat(d_skip, SSD_HEAD_DIM)[None, :],
    ]
    return w_perm, ssd_consts


def _constants():
    lane = jnp.arange(LANES)
    hl = lane % HEAD_DIM
    half = ROT_DIM // 2
    freq = ROPE_THETA ** (-jnp.arange(0, ROT_DIM, 2, dtype=F32) / ROT_DIM)
    invf = jnp.where(hl < ROT_DIM, freq[hl % half], 0.0).astype(F32)[None, :]
    sgn = jnp.where(hl < half, -1.0, jnp.where(hl < ROT_DIM, 1.0, 0.0)).astype(F32)[None, :]
    hrow = jnp.arange(LANES)[:, None]
    e64 = (hrow == (jnp.arange(SSD_WIDTH)[None, :] // SSD_HEAD_DIM)).astype(BF16)
    e128 = (hrow == (jnp.arange(SSD_HEADS * LANES)[None, :] // LANES)).astype(BF16)
    tril = (jnp.arange(SSD_CHUNK)[:, None] >= jnp.arange(SSD_CHUNK)[None, :]).astype(BF16)
    return invf, sgn, e64, e128, tril


def kernel(x, positions, norm_mix_w, w_in, attn_sinks, conv_w, conv_b, dt_bias, a_log, d_skip, ssd_norm_w,
           w_attn_o, w_ssd_o, w_out, norm_ffn_w, peer_wq, peer_keys, peer_u, peer_v, norm_final_w):
    B, S, D = x.shape
    T = B * S
    assert w_in.shape[0] == 1, "single-layer block: the final norm is fused into the PEER kernel"
    l = 0
    invf, sgn, e64, e128, tril = _constants()
    pos_col = positions.reshape(T, 1)
    xf = x.reshape(T, D)
    w_perm, ssd_consts = _layer_weights(w_in[l], conv_w[l], conv_b[l], dt_bias[l], a_log[l], d_skip[l])
    proj = _inproj(xf, norm_mix_w[l][None, :], w_perm)
    attn_o = _attention(proj, pos_col, invf, sgn, attn_sinks[l], w_attn_o[l].astype(BF16), B, S)
    yn = _ssd(proj, ssd_consts + [ssd_norm_w[l][None, :], e64, e128, tril], B, S)
    x1 = _merge(xf, attn_o, yn, proj, w_ssd_o[l].astype(BF16), w_out[l].astype(BF16))
    keys = peer_keys[l].reshape(PEER_HEADS * 2, PEER_N_KEYS, PEER_HALF).astype(BF16)
    idx, gates = _route(x1, norm_ffn_w[l][None, :], peer_wq[l].T.astype(BF16), keys)
    uv3 = jnp.concatenate([peer_u[l].reshape(-1, NCH, LANES), peer_v[l].reshape(-1, NCH, LANES)], axis=1)
    r3 = lambda a: a.reshape(1, NCH, LANES)
    out = _peer(idx, gates, x1.reshape(T, NCH, LANES), r3(norm_ffn_w[l]), r3(norm_final_w), uv3)
    return out.reshape(B, S, D)
```

```python
import functools
import math

import jax
import jax.numpy as jnp
from jax import lax
from jax.experimental import pallas as pl
from jax.experimental.pallas import tpu as pltpu

F32 = jnp.float32
BF16 = jnp.bfloat16
I32 = jnp.int32

D_MODEL = 1024
N_Q_HEADS = 16
N_KV_HEADS = 4
HEAD_DIM = 64
ATTN_WIDTH = N_Q_HEADS * HEAD_DIM
KV_WIDTH = N_KV_HEADS * HEAD_DIM
WINDOW = 128
ROT_DIM = HEAD_DIM // 4
ROPE_THETA = 500000.0
SSD_WIDTH = 2 * D_MODEL
SSD_HEAD_DIM = 64
SSD_HEADS = SSD_WIDTH // SSD_HEAD_DIM
SSD_GROUPS = 4
SSD_STATE = 128
SSD_CONV = 4
SSD_CHUNK = 128
BC_WIDTH = SSD_GROUPS * SSD_STATE
PEER_HEADS = 8
PEER_N_KEYS = 128
PEER_TOPK = 16
PEER_HALF = 128
PEER_HK = PEER_HEADS * PEER_TOPK
EPS = 1e-6

LANES = 128
VMEM_LIMIT = 56 * 1024 * 1024

COL_Z = 0
COL_XS = COL_Z + SSD_WIDTH
COL_BM = COL_XS + SSD_WIDTH
COL_CM = COL_BM + BC_WIDTH
COL_GA = COL_CM + BC_WIDTH
COL_GS = COL_GA + D_MODEL
COL_Q = COL_GS + D_MODEL
COL_K = COL_Q + ATTN_WIDTH
COL_V = COL_K + 2 * KV_WIDTH
COL_DT = COL_V + 2 * KV_WIDTH
PROJ_TN = 2432
PROJ_COLS = 4 * PROJ_TN
assert COL_DT + LANES <= PROJ_COLS


def _cparams(*sem):
    return pltpu.CompilerParams(dimension_semantics=sem, vmem_limit_bytes=VMEM_LIMIT)


def _rms(x, w):
    return x * lax.rsqrt(jnp.mean(x * x, axis=-1, keepdims=True) + EPS) * w


def _inproj_body(x_ref, nw_ref, w_ref, o_ref):
    h = _rms(x_ref[...], nw_ref[...])
    o_ref[...] = jnp.dot(h.astype(BF16), w_ref[...], preferred_element_type=F32)


def _inproj(xf, norm_w, w_perm, tm=512):
    T = xf.shape[0]
    return pl.pallas_call(
        _inproj_body,
        grid=(PROJ_COLS // PROJ_TN, T // tm),
        in_specs=[
            pl.BlockSpec((tm, D_MODEL), lambda n, m: (m, 0)),
            pl.BlockSpec((1, D_MODEL), lambda n, m: (0, 0)),
            pl.BlockSpec((D_MODEL, PROJ_TN), lambda n, m: (0, n)),
        ],
        out_specs=pl.BlockSpec((tm, PROJ_TN), lambda n, m: (m, n)),
        out_shape=jax.ShapeDtypeStruct((T, PROJ_COLS), F32),
        compiler_params=_cparams("arbitrary", "arbitrary"),
        name="inproj",
    )(xf, norm_w, w_perm)


def _attn_body(q_ref, kc_ref, vc_ref, kp_ref, vp_ref, pos_ref, posp_ref, invf_ref, sgn_ref,
               sink_ref, wo_ref, o_ref, *, tq):
    i = pl.program_id(1)
    lane = lax.broadcasted_iota(I32, (1, LANES), 1)
    lo8 = (lane % HEAD_DIM) < (ROT_DIM // 2)
    mlo = lane < HEAD_DIM

    def rope(t, pos):
        ang = pos.astype(F32) * invf_ref[...]
        cs = jnp.cos(ang)
        sn = jnp.sin(ang) * sgn_ref[...]
        outs = []
        for j in range(t.shape[1] // LANES):
            tj = t[:, j * LANES:(j + 1) * LANES]
            sh = jnp.where(lo8, pltpu.roll(tj, LANES - ROT_DIM // 2, 1), pltpu.roll(tj, ROT_DIM // 2, 1))
            outs.append(tj * cs + sh * sn)
        return jnp.concatenate(outs, axis=1)

    pos = pos_ref[...]
    qrot = rope(q_ref[...], pos)
    kfull = jnp.concatenate([rope(kp_ref[...], posp_ref[...]), rope(kc_ref[...], pos)], axis=0).astype(BF16)
    vfull = jnp.concatenate([vp_ref[...], vc_ref[...]], axis=0).astype(BF16)

    qi = lax.broadcasted_iota(I32, (WINDOW, 2 * WINDOW), 0)
    kj = lax.broadcasted_iota(I32, (WINDOW, 2 * WINDOW), 1)
    rel = WINDOW + qi - kj
    band = (rel >= 0) & (rel < WINDOW)
    rgrp = lax.broadcasted_iota(I32, (4 * WINDOW, 1), 0) // WINDOW
    scale = HEAD_DIM ** -0.5

    for c in range(tq // WINDOW):
        valid = band
        if c == 0:
            valid = band & ((kj >= WINDOW) | (i > 0))
        valid4 = jnp.concatenate([valid] * 4, axis=0)
        tiles = []
        for h in range(N_KV_HEADS):
            kh = kfull[c * WINDOW:c * WINDOW + 2 * WINDOW, h * LANES:(h + 1) * LANES]
            vh = vfull[c * WINDOW:c * WINDOW + 2 * WINDOW, h * LANES:(h + 1) * LANES]
            t0 = qrot[c * WINDOW:(c + 1) * WINDOW, (2 * h) * LANES:(2 * h + 1) * LANES]
            t1 = qrot[c * WINDOW:(c + 1) * WINDOW, (2 * h + 1) * LANES:(2 * h + 2) * LANES]
            qg = jnp.concatenate([jnp.where(mlo, t0, 0.0), jnp.where(mlo, 0.0, t0),
                                  jnp.where(mlo, t1, 0.0), jnp.where(mlo, 0.0, t1)], axis=0).astype(BF16)
            s = lax.dot_general(qg, kh, (((1,), (1,)), ((), ())), preferred_element_type=F32) * scale
            s = jnp.where(valid4, s, -jnp.inf)
            sk = jnp.where(rgrp == 0, sink_ref[4 * h],
                           jnp.where(rgrp == 1, sink_ref[4 * h + 1],
                                     jnp.where(rgrp == 2, sink_ref[4 * h + 2], sink_ref[4 * h + 3])))
            m = jnp.maximum(jnp.max(s, axis=-1, keepdims=True), sk)
            p = jnp.exp(s - m)
            den = jnp.sum(p, axis=-1, keepdims=True) + jnp.exp(sk - m)
            o = jnp.dot(p.astype(BF16), vh, preferred_element_type=F32) / den
            tiles.append(jnp.where(mlo, o[0:WINDOW], o[WINDOW:2 * WINDOW]))
            tiles.append(jnp.where(mlo, o[2 * WINDOW:3 * WINDOW], o[3 * WINDOW:4 * WINDOW]))
        attn = jnp.concatenate(tiles, axis=1).astype(BF16)
        o_ref[c * WINDOW:(c + 1) * WINDOW, :] = jnp.dot(attn, wo_ref[...], preferred_element_type=F32)


def _attention(proj, pos_col, invf, sgn, sinks, w_o, B, S, tq=512):
    T = B * S
    nq = S // tq
    nb = S // WINDOW
    r = tq // WINDOW
    cur = lambda col: (lambda b, i: (b * nq + i, col))
    prev = lambda col: (lambda b, i: (b * nb + jnp.maximum(i * r - 1, 0), col))
    return pl.pallas_call(
        functools.partial(_attn_body, tq=tq),
        grid=(B, nq),
        in_specs=[
            pl.BlockSpec((tq, ATTN_WIDTH), cur(COL_Q // ATTN_WIDTH)),
            pl.BlockSpec((tq, 2 * KV_WIDTH), cur(COL_K // (2 * KV_WIDTH))),
            pl.BlockSpec((tq, 2 * KV_WIDTH), cur(COL_V // (2 * KV_WIDTH))),
            pl.BlockSpec((WINDOW, 2 * KV_WIDTH), prev(COL_K // (2 * KV_WIDTH))),
            pl.BlockSpec((WINDOW, 2 * KV_WIDTH), prev(COL_V // (2 * KV_WIDTH))),
            pl.BlockSpec((tq, 1), cur(0)),
            pl.BlockSpec((WINDOW, 1), prev(0)),
            pl.BlockSpec((1, LANES), lambda b, i: (0, 0)),
            pl.BlockSpec((1, LANES), lambda b, i: (0, 0)),
            pl.BlockSpec(memory_space=pltpu.SMEM),
            pl.BlockSpec((ATTN_WIDTH, D_MODEL), lambda b, i: (0, 0)),
        ],
        out_specs=pl.BlockSpec((tq, D_MODEL), cur(0)),
        out_shape=jax.ShapeDtypeStruct((T, D_MODEL), F32),
        compiler_params=_cparams("arbitrary", "arbitrary"),
        name="attention",
    )(proj, proj, proj, proj, proj, pos_col, pos_col, invf, sgn, sinks, w_o)


def _split3(a):
    hi = a.astype(BF16)
    r1 = a - hi.astype(F32)
    mid = r1.astype(BF16)
    lo = (r1 - mid.astype(F32)).astype(BF16)
    return hi, mid, lo


def _sel_dot(a, e):
    hi, mid, lo = _split3(a)
    d = lambda u: jnp.dot(u, e, preferred_element_type=F32)
    return d(hi) + d(mid) + d(lo)


def _ssd_body(z_ref, xs_ref, bm_ref, cm_ref, dt_ref, cwx_ref, cwb_ref, cwc_ref, cbx_ref, cbb_ref, cbc_ref,
              dtb_ref, alog_ref, dsk_ref, nw_ref, e64_ref, e128_ref, tril_ref, o_ref,
              state, tail_x, tail_b, tail_c):
    L = SSD_CHUNK

    @pl.when(pl.program_id(1) == 0)
    def _():
        state[...] = jnp.zeros_like(state)
        tail_x[...] = jnp.zeros_like(tail_x)
        tail_b[...] = jnp.zeros_like(tail_b)
        tail_c[...] = jnp.zeros_like(tail_c)

    row8 = lax.broadcasted_iota(I32, (8, 1), 0)

    def conv_silu(u, tail_ref, w_ref, b_ref):
        tail = tail_ref[...]
        acc = u * w_ref[SSD_CONV - 1:SSD_CONV, :] + b_ref[...]
        for j in range(1, SSD_CONV):
            ru = pltpu.roll(u, j, 0)
            head = jnp.where(row8 < j, pltpu.roll(tail, j, 0), ru[0:8])
            sh = jnp.concatenate([head, ru[8:]], axis=0)
            acc = acc + sh * w_ref[SSD_CONV - 1 - j:SSD_CONV - j, :]
        tail_ref[...] = u[L - 8:L]
        return acc * jax.nn.sigmoid(acc)

    xs = conv_silu(xs_ref[...], tail_x, cwx_ref, cbx_ref)
    bm = conv_silu(bm_ref[...], tail_b, cwb_ref, cbb_ref)
    cm = conv_silu(cm_ref[...], tail_c, cwc_ref, cbc_ref)

    dt = jax.nn.softplus(dt_ref[...] + dtb_ref[...])
    dA = dt * (-jnp.exp(alog_ref[...]))
    hi, mid, lo = _split3(dA)
    tril = tril_ref[...]
    cs = lambda u: jnp.dot(tril, u, preferred_element_type=F32)
    acum = cs(hi) + cs(mid) + cs(lo)
    acum_t = acum.T
    e64 = e64_ref[...]
    dt_x = _sel_dot(dt, e64)
    acum_x = _sel_dot(acum, e64)
    tot_x = acum_x[L - 1:L, :]
    xdt = xs * dt_x
    xdt_b = xdt.astype(BF16)
    wx_b = (jnp.exp(tot_x - acum_x) * xdt).astype(BF16)
    eac_x = jnp.exp(acum_x)
    etot_x = jnp.exp(tot_x)
    cm_b = cm.astype(BF16)
    bm_b = bm.astype(BF16)

    ti = lax.broadcasted_iota(I32, (L, L), 0)
    si = lax.broadcasted_iota(I32, (L, L), 1)
    causal = ti >= si
    lane = lax.broadcasted_iota(I32, (1, LANES), 1)
    mlo = lane < SSD_HEAD_DIM
    GW = SSD_WIDTH // SSD_GROUPS
    ys = []
    for g in range(SSD_GROUPS):
        cmg = cm_b[:, g * SSD_STATE:(g + 1) * SSD_STATE]
        bmg = bm_b[:, g * SSD_STATE:(g + 1) * SSD_STATE]
        cb = lax.dot_general(cmg, bmg, (((1,), (1,)), ((), ())), preferred_element_type=F32)
        colx = _sel_dot(acum, e128_ref[:, g * 8 * LANES:(g + 1) * 8 * LANES])
        tiles = []
        for j in range(4):
            xt = xdt_b[:, g * GW + j * LANES:g * GW + (j + 1) * LANES]
            acc = None
            for half in range(2):
                r = 2 * j + half
                h = 8 * g + r
                seg = colx[:, r * LANES:(r + 1) * LANES] - acum_t[h:h + 1, :]
                dec = jnp.exp(jnp.where(causal, seg, -jnp.inf))
                mm = (cb * dec).astype(BF16)
                xm = jnp.where(mlo, xt, 0.0) if half == 0 else jnp.where(mlo, 0.0, xt)
                y = jnp.dot(mm, xm.astype(BF16), preferred_element_type=F32)
                acc = y if acc is None else acc + y
            tiles.append(acc)
        y_intra = jnp.concatenate(tiles, axis=1)
        st = state[:, g * GW:(g + 1) * GW]
        y_inter = jnp.dot(cmg, st.astype(BF16), preferred_element_type=F32) * eac_x[:, g * GW:(g + 1) * GW]
        ys.append(y_intra + y_inter)
        bmt = bm[:, g * SSD_STATE:(g + 1) * SSD_STATE].T.astype(BF16)
        state[:, g * GW:(g + 1) * GW] = st * etot_x[:, g * GW:(g + 1) * GW] + jnp.dot(
            bmt, wx_b[:, g * GW:(g + 1) * GW], preferred_element_type=F32)

    y = jnp.concatenate(ys, axis=1) + dsk_ref[...] * xs
    z = z_ref[...]
    y = y * (z * jax.nn.sigmoid(z))
    outs = []
    for g in range(SSD_GROUPS):
        outs.append(_rms(y[:, g * GW:(g + 1) * GW], nw_ref[:, g * GW:(g + 1) * GW]))
    o_ref[...] = jnp.concatenate(outs, axis=1).astype(BF16)


def _ssd(proj, consts, B, S):
    T = B * S
    L = SSD_CHUNK
    nc = S // L
    blk = lambda w, col: pl.BlockSpec((L, w), lambda b, c: (b * nc + c, col))
    full = lambda a: pl.BlockSpec(a.shape, lambda b, c: (0,) * a.ndim)
    return pl.pallas_call(
        _ssd_body,
        grid=(B, nc),
        in_specs=[
            blk(SSD_WIDTH, COL_Z // SSD_WIDTH),
            blk(SSD_WIDTH, COL_XS // SSD_WIDTH),
            blk(BC_WIDTH, COL_BM // BC_WIDTH),
            blk(BC_WIDTH, COL_CM // BC_WIDTH),
            blk(LANES, COL_DT // LANES),
        ] + [full(a) for a in consts],
        out_specs=pl.BlockSpec((L, SSD_WIDTH), lambda b, c: (b * nc + c, 0)),
        out_shape=jax.ShapeDtypeStruct((T, SSD_WIDTH), BF16),
        scratch_shapes=[
            pltpu.VMEM((SSD_STATE, SSD_WIDTH), F32),
            pltpu.VMEM((8, SSD_WIDTH), F32),
            pltpu.VMEM((8, BC_WIDTH), F32),
            pltpu.VMEM((8, BC_WIDTH), F32),
        ],
        compiler_params=_cparams("arbitrary", "arbitrary"),
        name="ssd",
    )(proj, proj, proj, proj, proj, *consts)


def _merge_body(x_ref, at_ref, yn_ref, ga_ref, gs_ref, wso_ref, wout_ref, o_ref):
    ssd = jnp.dot(yn_ref[...], wso_ref[...], preferred_element_type=F32)
    m = jax.nn.sigmoid(ga_ref[...]) * at_ref[...] + jax.nn.sigmoid(gs_ref[...]) * ssd
    o_ref[...] = x_ref[...] + jnp.dot(m.astype(BF16), wout_ref[...], preferred_element_type=F32)


def _merge(xf, attn_o, yn, proj, w_ssd_o, w_out, tm=512):
    T = xf.shape[0]
    row = lambda w, col: pl.BlockSpec((tm, w), lambda m: (m, col))
    return pl.pallas_call(
        _merge_body,
        grid=(T // tm,),
        in_specs=[
            row(D_MODEL, 0), row(D_MODEL, 0), row(SSD_WIDTH, 0),
            row(D_MODEL, COL_GA // D_MODEL), row(D_MODEL, COL_GS // D_MODEL),
            pl.BlockSpec((SSD_WIDTH, D_MODEL), lambda m: (0, 0)),
            pl.BlockSpec((D_MODEL, D_MODEL), lambda m: (0, 0)),
        ],
        out_specs=row(D_MODEL, 0),
        out_shape=jax.ShapeDtypeStruct((T, D_MODEL), F32),
        compiler_params=_cparams("arbitrary"),
        name="merge",
    )(xf, attn_o, yn, proj, proj, w_ssd_o, w_out)


def _top16(ss, payloads=None):
    n = ss[0].shape[0]
    iota = lax.broadcasted_iota(I32, ss[0].shape, 0).astype(F32)
    ss = list(ss)
    vals = [[] for _ in ss]
    sel = [[] for _ in ss]
    for _ in range(PEER_TOPK):
        for c in range(len(ss)):
            s = ss[c]
            m = jnp.max(s, axis=0, keepdims=True)
            tied = jnp.where(s == m, iota, float(n))
            am = jnp.min(tied, axis=0, keepdims=True)
            hit = tied == am
            vals[c].append(m)
            sel[c].append(am if payloads is None
                          else jnp.max(jnp.where(hit, payloads[c], -1.0), axis=0, keepdims=True))
            ss[c] = jnp.where(hit, -jnp.inf, s)
    cat = lambda rows: jnp.concatenate(rows, axis=0)
    return [cat(v) for v in vals], [cat(v) for v in sel]


def _pair_candidates(v1, i1, v2, i2):
    cands, ecands = [], []
    for i in range(PEER_TOPK):
        nj = PEER_TOPK // (i + 1)
        njp = 8 * ((nj + 7) // 8)
        cv = v1[i:i + 1] + v2[0:njp]
        if njp > nj:
            cv = jnp.where(lax.broadcasted_iota(I32, cv.shape, 0) < nj, cv, -jnp.inf)
        cands.append(cv)
        ecands.append(i1[i:i + 1] * PEER_N_KEYS + i2[0:njp])
    return jnp.concatenate(cands, axis=0), jnp.concatenate(ecands, axis=0)


NCH = D_MODEL // LANES
ROWGRP = PEER_HK // 8


def _rms3(x, w):
    ms = jnp.sum(jnp.sum(x * x, axis=2, keepdims=True), axis=1, keepdims=True) * (1.0 / D_MODEL)
    return x * lax.rsqrt(ms + EPS) * w


def _sublane_totals(p):
    sub = lax.broadcasted_iota(I32, p[0].shape, 1)

    def comb(a, b, h, phase):
        m = ((sub - phase) & (2 * h - 1)) < h
        return jnp.where(m, a, b) + pltpu.roll(jnp.where(m, b, a), h, 1)

    c = comb(p[0], p[4], 4, 5)
    d = comb(p[2], p[6], 4, 7)
    e = comb(p[1], p[5], 4, 6)
    f = comb(p[3], p[7], 4, 0)
    return comb(comb(c, d, 2, 3), comb(e, f, 2, 0), 1, 0)


ROUTE_BLOCK = 128
ROUTE_LAG = 9
RING = 3


def _peer_body(x_ref, x2f_ref, x2_ref, nfw2_ref, wqt_ref, keys_ref, nfw_ref, nlw_ref, uv_ref, o_ref,
               gbuf0, gbuf1, sem, hn_scr, cs_scr, acc, hb_scr, it_scr, gt_scr, ring_i, ring_g, sidx, ssem, *, tb):
    i = pl.program_id(0)
    n = pl.num_programs(0)
    per_blk = ROUTE_BLOCK // tb
    n_units = (n // per_blk) * PEER_HEADS

    def route_unit(u, xsrc_ref):
        h = u % PEER_HEADS
        blk = u // PEER_HEADS

        @pl.when(h == 0)
        def _():
            hb_scr[...] = _rms(xsrc_ref[...], nfw2_ref[...]).astype(BF16)

        r0 = pl.multiple_of(h * 2 * PEER_HALF, 2 * PEER_HALF)
        qt = lax.dot_general(wqt_ref[pl.ds(r0, 2 * PEER_HALF), :], hb_scr[...], (((1,), (1,)), ((), ())),
                             preferred_element_type=F32).astype(BF16)
        s1 = jnp.dot(keys_ref[2 * h], qt[0:PEER_HALF], preferred_element_type=F32)
        s2 = jnp.dot(keys_ref[2 * h + 1], qt[PEER_HALF:2 * PEER_HALF], preferred_element_type=F32)
        v, ix = _top16([s1, s2])
        cand, ecand = _pair_candidates(v[0], ix[0], v[1], ix[1])
        sc, e = _top16([cand], [ecand])
        p = jnp.exp(sc[0] - jnp.max(sc[0], axis=0, keepdims=True))
        o = pl.multiple_of(h * PEER_TOPK, PEER_TOPK)
        it_scr[pl.ds(o, PEER_TOPK), :] = e[0]
        gt_scr[pl.ds(o, PEER_TOPK), :] = p / jnp.sum(p, axis=0, keepdims=True)

        @pl.when(h == PEER_HEADS - 1)
        def _():
            slot = blk % RING
            ring_i[slot] = it_scr[...].T.astype(I32)
            ring_g[slot] = gt_scr[...].T

    def idx_copy(blk16, par):
        b = jnp.minimum(blk16, n - 1)
        r0 = pl.multiple_of((b % per_blk) * tb, tb)
        return pltpu.make_async_copy(ring_i.at[(b // per_blk) % RING, pl.ds(r0, tb), :], sidx.at[par], ssem.at[par])

    def wait_buf(buf, s):
        pltpu.make_async_copy(buf, buf, s).wait()

    @pl.when(i == 0)
    def _():
        def body(u, c):
            route_unit(u, x2f_ref)
            return c
        lax.fori_loop(0, PEER_HEADS, body, 0)
        for u in range(PEER_HEADS, ROUTE_LAG):
            route_unit(u, x2_ref)
        cp = idx_copy(0, 0)
        cp.start()
        cp.wait()

        def rows(t, c):
            for k in range(PEER_HK):
                pltpu.make_async_copy(uv_ref.at[sidx[0, t, k]], gbuf0.at[t * PEER_HK + k],
                                      sem.at[0]).start(priority=k % 2)
            return c
        lax.fori_loop(0, tb, rows, 0)

    shape = (ROWGRP, 8, LANES)
    eye = lax.broadcasted_iota(I32, shape, 2) == (lax.broadcasted_iota(I32, shape, 1) * ROWGRP
                                                  + lax.broadcasted_iota(I32, shape, 0))
    g_slot = (i // per_blk) % RING
    g_row0 = pl.multiple_of((i % per_blk) * tb, tb)

    def step(cur, cur_sem, nxt, nxt_sem, par):
        nxt_idx = idx_copy(i + 1, par)
        nxt_idx.start()
        wait_buf(cur, cur_sem)
        x = x_ref[...]
        hn_scr[...] = _rms3(x, nfw_ref[...])
        nxt_idx.wait()

        for t in range(tb):
            for k in range(PEER_HK):
                pltpu.make_async_copy(uv_ref.at[sidx[par, t, k]], nxt.at[t * PEER_HK + k],
                                      nxt_sem).start(priority=k % 2)
            base = t * PEER_HK
            hn = hn_scr[t]
            prods = [cur[pl.ds(base + j * ROWGRP, ROWGRP), 0:NCH, :] * hn for j in range(8)]
            a = jnp.sum(_sublane_totals(prods), axis=-1, keepdims=True)
            gcol = jnp.sum(jnp.where(eye, ring_g[g_slot, pl.ds(g_row0 + t, 1), :], 0.0), axis=-1, keepdims=True)
            cc = gcol * (0.5 * a * (1.0 + lax.erf(a * (2.0 ** -0.5))))
            cs_scr[...] = jnp.broadcast_to(cc, shape)
            parts = [None] * 4
            for k in range(PEER_HK):
                j, g = divmod(k, ROWGRP)
                term = cs_scr[g, pl.ds(j, 1), :] * cur[base + k, NCH:2 * NCH, :]
                parts[k % 4] = term if parts[k % 4] is None else parts[k % 4] + term
            acc[t] = (parts[0] + parts[1]) + (parts[2] + parts[3])
        o_ref[...] = _rms3(x + acc[...], nlw_ref[...])

    @pl.when(i % 2 == 0)
    def _():
        step(gbuf0, sem.at[0], gbuf1, sem.at[1], 1)

    @pl.when(i % 2 == 1)
    def _():
        step(gbuf1, sem.at[1], gbuf0, sem.at[0], 0)

    @pl.when(i + ROUTE_LAG < n_units)
    def _():
        route_unit(i + ROUTE_LAG, x2_ref)

    @pl.when(i == n - 1)
    def _():
        @pl.when(i % 2 == 0)
        def _():
            wait_buf(gbuf1, sem.at[1])

        @pl.when(i % 2 == 1)
        def _():
            wait_buf(gbuf0, sem.at[0])


def _peer(x1, nfw, nlw, wq_t, keys, uv3, tb=16):
    T = x1.shape[0]
    nblk = T // tb
    per_blk = ROUTE_BLOCK // tb
    n_rblk = T // ROUTE_BLOCK
    assert n_rblk >= 2 and T % ROUTE_BLOCK == 0
    tok = lambda m: (m, 0, 0)
    r3 = lambda a: a.reshape(1, NCH, LANES)
    full2 = lambda a: pl.BlockSpec(a.shape, lambda m: (0,) * a.ndim)
    return pl.pallas_call(
        functools.partial(_peer_body, tb=tb),
        grid=(nblk,),
        in_specs=[
            pl.BlockSpec((tb, NCH, LANES), tok),
            pl.BlockSpec((ROUTE_BLOCK, D_MODEL), lambda m: (0, 0)),
            pl.BlockSpec((ROUTE_BLOCK, D_MODEL),
                         lambda m: (jnp.minimum((m + ROUTE_LAG) // per_blk, n_rblk - 1), 0)),
            pl.BlockSpec((1, D_MODEL), lambda m: (0, 0)),
            full2(wq_t), full2(keys),
            pl.BlockSpec((1, NCH, LANES), lambda m: (0, 0, 0)),
            pl.BlockSpec((1, NCH, LANES), lambda m: (0, 0, 0)),
            pl.BlockSpec(memory_space=pl.ANY),
        ],
        out_specs=pl.BlockSpec((tb, NCH, LANES), tok),
        out_shape=jax.ShapeDtypeStruct((T, NCH, LANES), F32),
        scratch_shapes=[
            pltpu.VMEM((tb * PEER_HK, 2 * NCH, LANES), F32),
            pltpu.VMEM((tb * PEER_HK, 2 * NCH, LANES), F32),
            pltpu.SemaphoreType.DMA((2,)),
            pltpu.VMEM((tb, NCH, LANES), F32),
            pltpu.VMEM((ROWGRP, 8, LANES), F32),
            pltpu.VMEM((tb, NCH, LANES), F32),
            pltpu.VMEM((ROUTE_BLOCK, D_MODEL), BF16),
            pltpu.VMEM((PEER_HK, ROUTE_BLOCK), F32),
            pltpu.VMEM((PEER_HK, ROUTE_BLOCK), F32),
            pltpu.VMEM((RING, ROUTE_BLOCK, PEER_HK), I32),
            pltpu.VMEM((RING, ROUTE_BLOCK, PEER_HK), F32),
            pltpu.SMEM((2, tb, PEER_HK), I32),
            pltpu.SemaphoreType.DMA((2,)),
        ],
        compiler_params=_cparams("arbitrary"),
        name="peer",
    )(x1.reshape(T, NCH, LANES), x1, x1, nfw[None, :], wq_t, keys, r3(nfw), r3(nlw), uv3)


def _layer_weights(w_in, conv_w, conv_b, dt_bias, a_log, d_skip):
    D = D_MODEL
    q, k, v, z, xbc, dt, gates = jnp.split(
        w_in, [ATTN_WIDTH, ATTN_WIDTH + KV_WIDTH, ATTN_WIDTH + 2 * KV_WIDTH,
               ATTN_WIDTH + 2 * KV_WIDTH + SSD_WIDTH,
               ATTN_WIDTH + 2 * KV_WIDTH + SSD_WIDTH + SSD_WIDTH + 2 * BC_WIDTH,
               ATTN_WIDTH + 2 * KV_WIDTH + SSD_WIDTH + SSD_WIDTH + 2 * BC_WIDTH + SSD_HEADS], axis=1)
    dup = lambda t: jnp.concatenate([t.reshape(D, N_KV_HEADS, HEAD_DIM)] * 2, axis=-1).reshape(D, 2 * KV_WIDTH)
    pad = PROJ_COLS - (COL_DT + SSD_HEADS)
    w_perm = jnp.concatenate([z, xbc, gates, q, dup(k), dup(v), dt, jnp.zeros((D, pad), w_in.dtype)],
                             axis=1).astype(BF16)
    lanes_pad = lambda a: jnp.pad(a.reshape(1, SSD_HEADS), ((0, 0), (0, LANES - SSD_HEADS)))
    ssd_consts = [
        conv_w[:, :SSD_WIDTH], conv_w[:, SSD_WIDTH:SSD_WIDTH + BC_WIDTH], conv_w[:, SSD_WIDTH + BC_WIDTH:],
        conv_b[None, :SSD_WIDTH], conv_b[None, SSD_WIDTH:SSD_WIDTH + BC_WIDTH], conv_b[None, SSD_WIDTH + BC_WIDTH:],
        lanes_pad(dt_bias), lanes_pad(a_log),
        jnp.repeat(d_skip, SSD_HEAD_DIM)[None, :],
    ]
    return w_perm, ssd_consts


def _constants():
    lane = jnp.arange(LANES)
    hl = lane % HEAD_DIM
    half = ROT_DIM // 2
    freq = ROPE_THETA ** (-jnp.arange(0, ROT_DIM, 2, dtype=F32) / ROT_DIM)
    invf = jnp.where(hl < ROT_DIM, freq[hl % half], 0.0).astype(F32)[None, :]
    sgn = jnp.where(hl < half, -1.0, jnp.where(hl < ROT_DIM, 1.0, 0.0)).astype(F32)[None, :]
    hrow = jnp.arange(LANES)[:, None]
    e64 = (hrow == (jnp.arange(SSD_WIDTH)[None, :] // SSD_HEAD_DIM)).astype(BF16)
    e128 = (hrow == (jnp.arange(SSD_HEADS * LANES)[None, :] // LANES)).astype(BF16)
    tril = (jnp.arange(SSD_CHUNK)[:, None] >= jnp.arange(SSD_CHUNK)[None, :]).astype(BF16)
    return invf, sgn, e64, e128, tril


def kernel(x, positions, norm_mix_w, w_in, attn_sinks, conv_w, conv_b, dt_bias, a_log, d_skip, ssd_norm_w,
           w_attn_o, w_ssd_o, w_out, norm_ffn_w, peer_wq, peer_keys, peer_u, peer_v, norm_final_w):
    B, S, D = x.shape
    T = B * S
    assert w_in.shape[0] == 1, "single-layer block: the final norm is fused into the PEER kernel"
    l = 0
    invf, sgn, e64, e128, tril = _constants()
    pos_col = positions.reshape(T, 1)
    xf = x.reshape(T, D)
    w_perm, ssd_consts = _layer_weights(w_in[l], conv_w[l], conv_b[l], dt_bias[l], a_log[l], d_skip[l])
    proj = _inproj(xf, norm_mix_w[l][None, :], w_perm)
    attn_o = _attention(proj, pos_col, invf, sgn, attn_sinks[l], w_attn_o[l].astype(BF16), B, S)
    yn = _ssd(proj, ssd_consts + [ssd_norm_w[l][None, :], e64, e128, tril], B, S)
    x1 = _merge(xf, attn_o, yn, proj, w_ssd_o[l].astype(BF16), w_out[l].astype(BF16))
    keys = peer_keys[l].reshape(PEER_HEADS * 2, PEER_N_KEYS, PEER_HALF).astype(BF16)
    uv3 = jnp.concatenate([peer_u[l].reshape(-1, NCH, LANES), peer_v[l].reshape(-1, NCH, LANES)], axis=1)
    out = _peer(x1, norm_ffn_w[l], norm_final_w, peer_wq[l].T.astype(BF16), keys, uv3)
    return out.reshape(B, S, D)
```

```python
import functools
import math

import jax
import jax.numpy as jnp
from jax import lax
from jax.experimental import pallas as pl
from jax.experimental.pallas import tpu as pltpu

F32 = jnp.float32
BF16 = jnp.bfloat16
I32 = jnp.int32

D_MODEL = 1024
N_Q_HEADS = 16
N_KV_HEADS = 4
HEAD_DIM = 64
ATTN_WIDTH = N_Q_HEADS * HEAD_DIM
KV_WIDTH = N_KV_HEADS * HEAD_DIM
WINDOW = 128
ROT_DIM = HEAD_DIM // 4
ROPE_THETA = 500000.0
SSD_WIDTH = 2 * D_MODEL
SSD_HEAD_DIM = 64
SSD_HEADS = SSD_WIDTH // SSD_HEAD_DIM
SSD_GROUPS = 4
SSD_STATE = 128
SSD_CONV = 4
SSD_CHUNK = 128
BC_WIDTH = SSD_GROUPS * SSD_STATE
PEER_HEADS = 8
PEER_N_KEYS = 128
PEER_TOPK = 16
PEER_HALF = 128
PEER_HK = PEER_HEADS * PEER_TOPK
EPS = 1e-6

LANES = 128
VMEM_LIMIT = 56 * 1024 * 1024

COL_Z = 0
COL_XS = COL_Z + SSD_WIDTH
COL_BM = COL_XS + SSD_WIDTH
COL_CM = COL_BM + BC_WIDTH
COL_GA = COL_CM + BC_WIDTH
COL_GS = COL_GA + D_MODEL
COL_Q = COL_GS + D_MODEL
COL_K = COL_Q + ATTN_WIDTH
COL_V = COL_K + 2 * KV_WIDTH
COL_DT = COL_V + 2 * KV_WIDTH
PROJ_TN = 2432
PROJ_COLS = 4 * PROJ_TN
assert COL_DT + LANES <= PROJ_COLS


def _cparams(*sem):
    return pltpu.CompilerParams(dimension_semantics=sem, vmem_limit_bytes=VMEM_LIMIT)


def _rms(x, w):
    return x * lax.rsqrt(jnp.mean(x * x, axis=-1, keepdims=True) + EPS) * w


def _inproj_body(x_ref, nw_ref, w_ref, o_ref):
    h = _rms(x_ref[...], nw_ref[...])
    o_ref[...] = jnp.dot(h.astype(BF16), w_ref[...], preferred_element_type=F32)


def _inproj(xf, norm_w, w_perm, tm=512):
    T = xf.shape[0]
    return pl.pallas_call(
        _inproj_body,
        grid=(PROJ_COLS // PROJ_TN, T // tm),
        in_specs=[
            pl.BlockSpec((tm, D_MODEL), lambda n, m: (m, 0)),
            pl.BlockSpec((1, D_MODEL), lambda n, m: (0, 0)),
            pl.BlockSpec((D_MODEL, PROJ_TN), lambda n, m: (0, n)),
        ],
        out_specs=pl.BlockSpec((tm, PROJ_TN), lambda n, m: (m, n)),
        out_shape=jax.ShapeDtypeStruct((T, PROJ_COLS), F32),
        compiler_params=_cparams("arbitrary", "arbitrary"),
        name="inproj",
    )(xf, norm_w, w_perm)


def _attn_body(q_ref, kc_ref, vc_ref, kp_ref, vp_ref, pos_ref, posp_ref, invf_ref, sgn_ref,
               sink_ref, wo_ref, o_ref, *, tq):
    i = pl.program_id(1)
    lane = lax.broadcasted_iota(I32, (1, LANES), 1)
    lo8 = (lane % HEAD_DIM) < (ROT_DIM // 2)
    mlo = lane < HEAD_DIM

    def cos_sin(pos):
        ang = pos.astype(F32) * invf_ref[...]
        return jnp.cos(ang), jnp.sin(ang) * sgn_ref[...]

    def rope(t, cs_sn):
        cs, sn = cs_sn
        outs = []
        for j in range(t.shape[1] // LANES):
            tj = t[:, j * LANES:(j + 1) * LANES]
            sh = jnp.where(lo8, pltpu.roll(tj, LANES - ROT_DIM // 2, 1), pltpu.roll(tj, ROT_DIM // 2, 1))
            outs.append(tj * cs + sh * sn)
        return jnp.concatenate(outs, axis=1)

    trig = cos_sin(pos_ref[...])
    qrot = rope(q_ref[...], trig)
    kfull = jnp.concatenate([rope(kp_ref[...], cos_sin(posp_ref[...])), rope(kc_ref[...], trig)],
                            axis=0).astype(BF16)
    vfull = jnp.concatenate([vp_ref[...], vc_ref[...]], axis=0).astype(BF16)

    qi = lax.broadcasted_iota(I32, (WINDOW, 2 * WINDOW), 0)
    kj = lax.broadcasted_iota(I32, (WINDOW, 2 * WINDOW), 1)
    rel = WINDOW + qi - kj
    band = (rel >= 0) & (rel < WINDOW)
    rgrp = lax.broadcasted_iota(I32, (4 * WINDOW, 1), 0) // WINDOW
    scale = HEAD_DIM ** -0.5

    for c in range(tq // WINDOW):
        valid = band
        if c == 0:
            valid = band & ((kj >= WINDOW) | (i > 0))
        valid4 = jnp.concatenate([valid] * 4, axis=0)
        tiles = []
        for h in range(N_KV_HEADS):
            kh = kfull[c * WINDOW:c * WINDOW + 2 * WINDOW, h * LANES:(h + 1) * LANES]
            vh = vfull[c * WINDOW:c * WINDOW + 2 * WINDOW, h * LANES:(h + 1) * LANES]
            t0 = qrot[c * WINDOW:(c + 1) * WINDOW, (2 * h) * LANES:(2 * h + 1) * LANES]
            t1 = qrot[c * WINDOW:(c + 1) * WINDOW, (2 * h + 1) * LANES:(2 * h + 2) * LANES]
            qg = jnp.concatenate([jnp.where(mlo, t0, 0.0), jnp.where(mlo, 0.0, t0),
                                  jnp.where(mlo, t1, 0.0), jnp.where(mlo, 0.0, t1)], axis=0).astype(BF16)
            s = lax.dot_general(qg, kh, (((1,), (1,)), ((), ())), preferred_element_type=F32) * scale
            s = jnp.where(valid4, s, -jnp.inf)
            sk = jnp.where(rgrp == 0, sink_ref[4 * h],
                           jnp.where(rgrp == 1, sink_ref[4 * h + 1],
                                     jnp.where(rgrp == 2, sink_ref[4 * h + 2], sink_ref[4 * h + 3])))
            m = jnp.maximum(jnp.max(s, axis=-1, keepdims=True), sk)
            p = jnp.exp(s - m)
            den = jnp.sum(p, axis=-1, keepdims=True) + jnp.exp(sk - m)
            o = jnp.dot(p.astype(BF16), vh, preferred_element_type=F32) / den
            tiles.append(jnp.where(mlo, o[0:WINDOW], o[WINDOW:2 * WINDOW]))
            tiles.append(jnp.where(mlo, o[2 * WINDOW:3 * WINDOW], o[3 * WINDOW:4 * WINDOW]))
        attn = jnp.concatenate(tiles, axis=1).astype(BF16)
        o_ref[c * WINDOW:(c + 1) * WINDOW, :] = jnp.dot(attn, wo_ref[...], preferred_element_type=F32)


def _attention(proj, pos_col, invf, sgn, sinks, w_o, B, S, tq=512):
    T = B * S
    nq = S // tq
    nb = S // WINDOW
    r = tq // WINDOW
    cur = lambda col: (lambda b, i: (b * nq + i, col))
    prev = lambda col: (lambda b, i: (b * nb + jnp.maximum(i * r - 1, 0), col))
    return pl.pallas_call(
        functools.partial(_attn_body, tq=tq),
        grid=(B, nq),
        in_specs=[
            pl.BlockSpec((tq, ATTN_WIDTH), cur(COL_Q // ATTN_WIDTH)),
            pl.BlockSpec((tq, 2 * KV_WIDTH), cur(COL_K // (2 * KV_WIDTH))),
            pl.BlockSpec((tq, 2 * KV_WIDTH), cur(COL_V // (2 * KV_WIDTH))),
            pl.BlockSpec((WINDOW, 2 * KV_WIDTH), prev(COL_K // (2 * KV_WIDTH))),
            pl.BlockSpec((WINDOW, 2 * KV_WIDTH), prev(COL_V // (2 * KV_WIDTH))),
            pl.BlockSpec((tq, 1), cur(0)),
            pl.BlockSpec((WINDOW, 1), prev(0)),
            pl.BlockSpec((1, LANES), lambda b, i: (0, 0)),
            pl.BlockSpec((1, LANES), lambda b, i: (0, 0)),
            pl.BlockSpec(memory_space=pltpu.SMEM),
            pl.BlockSpec((ATTN_WIDTH, D_MODEL), lambda b, i: (0, 0)),
        ],
        out_specs=pl.BlockSpec((tq, D_MODEL), cur(0)),
        out_shape=jax.ShapeDtypeStruct((T, D_MODEL), F32),
        compiler_params=_cparams("arbitrary", "arbitrary"),
        name="attention",
    )(proj, proj, proj, proj, proj, pos_col, pos_col, invf, sgn, sinks, w_o)


def _split3(a):
    hi = a.astype(BF16)
    r1 = a - hi.astype(F32)
    mid = r1.astype(BF16)
    lo = (r1 - mid.astype(F32)).astype(BF16)
    return hi, mid, lo


def _sel_dot(a, e):
    hi, mid, lo = _split3(a)
    d = lambda u: jnp.dot(u, e, preferred_element_type=F32)
    return d(hi) + d(mid) + d(lo)


def _ssd_body(z_ref, xs_ref, bm_ref, cm_ref, dt_ref, cwx_ref, cwb_ref, cwc_ref, cbx_ref, cbb_ref, cbc_ref,
              dtb_ref, alog_ref, dsk_ref, nw_ref, e64_ref, e128_ref, tril_ref, o_ref,
              state, tail_x, tail_b, tail_c):
    L = SSD_CHUNK

    @pl.when(pl.program_id(1) == 0)
    def _():
        state[...] = jnp.zeros_like(state)
        tail_x[...] = jnp.zeros_like(tail_x)
        tail_b[...] = jnp.zeros_like(tail_b)
        tail_c[...] = jnp.zeros_like(tail_c)

    row8 = lax.broadcasted_iota(I32, (8, 1), 0)

    def conv_silu(u, tail_ref, w_ref, b_ref):
        tail = tail_ref[...]
        acc = u * w_ref[SSD_CONV - 1:SSD_CONV, :] + b_ref[...]
        for j in range(1, SSD_CONV):
            ru = pltpu.roll(u, j, 0)
            head = jnp.where(row8 < j, pltpu.roll(tail, j, 0), ru[0:8])
            sh = jnp.concatenate([head, ru[8:]], axis=0)
            acc = acc + sh * w_ref[SSD_CONV - 1 - j:SSD_CONV - j, :]
        tail_ref[...] = u[L - 8:L]
        return acc * jax.nn.sigmoid(acc)

    xs = conv_silu(xs_ref[...], tail_x, cwx_ref, cbx_ref)
    bm = conv_silu(bm_ref[...], tail_b, cwb_ref, cbb_ref)
    cm = conv_silu(cm_ref[...], tail_c, cwc_ref, cbc_ref)

    dt = jax.nn.softplus(dt_ref[...] + dtb_ref[...])
    dA = dt * (-jnp.exp(alog_ref[...]))
    hi, mid, lo = _split3(dA)
    tril = tril_ref[...]
    cs = lambda u: jnp.dot(tril, u, preferred_element_type=F32)
    acum = cs(hi) + cs(mid) + cs(lo)
    acum_t = acum.T
    e64 = e64_ref[...]
    dt_x = _sel_dot(dt, e64)
    acum_x = _sel_dot(acum, e64)
    tot_x = acum_x[L - 1:L, :]
    xdt = xs * dt_x
    xdt_b = xdt.astype(BF16)
    wx_b = (jnp.exp(tot_x - acum_x) * xdt).astype(BF16)
    eac_x = jnp.exp(acum_x)
    etot_x = jnp.exp(tot_x)
    cm_b = cm.astype(BF16)
    bm_b = bm.astype(BF16)

    ti = lax.broadcasted_iota(I32, (L, L), 0)
    si = lax.broadcasted_iota(I32, (L, L), 1)
    causal = ti >= si
    lane = lax.broadcasted_iota(I32, (1, LANES), 1)
    mlo = lane < SSD_HEAD_DIM
    GW = SSD_WIDTH // SSD_GROUPS
    ys = []
    for g in range(SSD_GROUPS):
        cmg = cm_b[:, g * SSD_STATE:(g + 1) * SSD_STATE]
        bmg = bm_b[:, g * SSD_STATE:(g + 1) * SSD_STATE]
        cb = lax.dot_general(cmg, bmg, (((1,), (1,)), ((), ())), preferred_element_type=F32)
        colx = _sel_dot(acum, e128_ref[:, g * 8 * LANES:(g + 1) * 8 * LANES])
        tiles = []
        for j in range(4):
            xt = xdt_b[:, g * GW + j * LANES:g * GW + (j + 1) * LANES]
            acc = None
            for half in range(2):
                r = 2 * j + half
                h = 8 * g + r
                seg = colx[:, r * LANES:(r + 1) * LANES] - acum_t[h:h + 1, :]
                dec = jnp.exp(jnp.where(causal, seg, -jnp.inf))
                mm = (cb * dec).astype(BF16)
                xm = jnp.where(mlo, xt, 0.0) if half == 0 else jnp.where(mlo, 0.0, xt)
                y = jnp.dot(mm, xm.astype(BF16), preferred_element_type=F32)
                acc = y if acc is None else acc + y
            tiles.append(acc)
        y_intra = jnp.concatenate(tiles, axis=1)
        st = state[:, g * GW:(g + 1) * GW]
        y_inter = jnp.dot(cmg, st.astype(BF16), preferred_element_type=F32) * eac_x[:, g * GW:(g + 1) * GW]
        ys.append(y_intra + y_inter)
        bmt = bm[:, g * SSD_STATE:(g + 1) * SSD_STATE].T.astype(BF16)
        state[:, g * GW:(g + 1) * GW] = st * etot_x[:, g * GW:(g + 1) * GW] + jnp.dot(
            bmt, wx_b[:, g * GW:(g + 1) * GW], preferred_element_type=F32)

    y = jnp.concatenate(ys, axis=1) + dsk_ref[...] * xs
    z = z_ref[...]
    y = y * (z * jax.nn.sigmoid(z))
    outs = []
    for g in range(SSD_GROUPS):
        outs.append(_rms(y[:, g * GW:(g + 1) * GW], nw_ref[:, g * GW:(g + 1) * GW]))
    o_ref[...] = jnp.concatenate(outs, axis=1).astype(BF16)


def _ssd(proj, consts, B, S):
    T = B * S
    L = SSD_CHUNK
    nc = S // L
    blk = lambda w, col: pl.BlockSpec((L, w), lambda b, c: (b * nc + c, col))
    full = lambda a: pl.BlockSpec(a.shape, lambda b, c: (0,) * a.ndim)
    return pl.pallas_call(
        _ssd_body,
        grid=(B, nc),
        in_specs=[
            blk(SSD_WIDTH, COL_Z // SSD_WIDTH),
            blk(SSD_WIDTH, COL_XS // SSD_WIDTH),
            blk(BC_WIDTH, COL_BM // BC_WIDTH),
            blk(BC_WIDTH, COL_CM // BC_WIDTH),
            blk(LANES, COL_DT // LANES),
        ] + [full(a) for a in consts],
        out_specs=pl.BlockSpec((L, SSD_WIDTH), lambda b, c: (b * nc + c, 0)),
        out_shape=jax.ShapeDtypeStruct((T, SSD_WIDTH), BF16),
        scratch_shapes=[
            pltpu.VMEM((SSD_STATE, SSD_WIDTH), F32),
            pltpu.VMEM((8, SSD_WIDTH), F32),
            pltpu.VMEM((8, BC_WIDTH), F32),
            pltpu.VMEM((8, BC_WIDTH), F32),
        ],
        compiler_params=_cparams("arbitrary", "arbitrary"),
        name="ssd",
    )(proj, proj, proj, proj, proj, *consts)


def _merge_body(x_ref, at_ref, yn_ref, ga_ref, gs_ref, wso_ref, wout_ref, o_ref):
    ssd = jnp.dot(yn_ref[...], wso_ref[...], preferred_element_type=F32)
    m = jax.nn.sigmoid(ga_ref[...]) * at_ref[...] + jax.nn.sigmoid(gs_ref[...]) * ssd
    o_ref[...] = x_ref[...] + jnp.dot(m.astype(BF16), wout_ref[...], preferred_element_type=F32)


def _merge(xf, attn_o, yn, proj, w_ssd_o, w_out, tm=512):
    T = xf.shape[0]
    row = lambda w, col: pl.BlockSpec((tm, w), lambda m: (m, col))
    return pl.pallas_call(
        _merge_body,
        grid=(T // tm,),
        in_specs=[
            row(D_MODEL, 0), row(D_MODEL, 0), row(SSD_WIDTH, 0),
            row(D_MODEL, COL_GA // D_MODEL), row(D_MODEL, COL_GS // D_MODEL),
            pl.BlockSpec((SSD_WIDTH, D_MODEL), lambda m: (0, 0)),
            pl.BlockSpec((D_MODEL, D_MODEL), lambda m: (0, 0)),
        ],
        out_specs=row(D_MODEL, 0),
        out_shape=jax.ShapeDtypeStruct((T, D_MODEL), F32),
        compiler_params=_cparams("arbitrary"),
        name="merge",
    )(xf, attn_o, yn, proj, proj, w_ssd_o, w_out)


def _top16(ss, payloads=None):
    n = ss[0].shape[0]
    iota = lax.broadcasted_iota(I32, ss[0].shape, 0).astype(F32)
    ss = list(ss)
    vals = [[] for _ in ss]
    sel = [[] for _ in ss]
    for _ in range(PEER_TOPK):
        for c in range(len(ss)):
            s = ss[c]
            m = jnp.max(s, axis=0, keepdims=True)
            tied = jnp.where(s == m, iota, float(n))
            am = jnp.min(tied, axis=0, keepdims=True)
            hit = tied == am
            vals[c].append(m)
            sel[c].append(am if payloads is None
                          else jnp.max(jnp.where(hit, payloads[c], -1.0), axis=0, keepdims=True))
            ss[c] = jnp.where(hit, -jnp.inf, s)
    cat = lambda rows: jnp.concatenate(rows, axis=0)
    return [cat(v) for v in vals], [cat(v) for v in sel]


def _pair_candidates(v1, i1, v2, i2):
    cands, ecands = [], []
    for i in range(PEER_TOPK):
        nj = PEER_TOPK // (i + 1)
        njp = 8 * ((nj + 7) // 8)
        cv = v1[i:i + 1] + v2[0:njp]
        if njp > nj:
            cv = jnp.where(lax.broadcasted_iota(I32, cv.shape, 0) < nj, cv, -jnp.inf)
        cands.append(cv)
        ecands.append(i1[i:i + 1] * PEER_N_KEYS + i2[0:njp])
    return jnp.concatenate(cands, axis=0), jnp.concatenate(ecands, axis=0)


NCH = D_MODEL // LANES
ROWGRP = PEER_HK // 8


def _rms3(x, w):
    ms = jnp.sum(jnp.sum(x * x, axis=2, keepdims=True), axis=1, keepdims=True) * (1.0 / D_MODEL)
    return x * lax.rsqrt(ms + EPS) * w


def _sublane_totals(p):
    sub = lax.broadcasted_iota(I32, p[0].shape, 1)

    def comb(a, b, h, phase):
        m = ((sub - phase) & (2 * h - 1)) < h
        return jnp.where(m, a, b) + pltpu.roll(jnp.where(m, b, a), h, 1)

    c = comb(p[0], p[4], 4, 5)
    d = comb(p[2], p[6], 4, 7)
    e = comb(p[1], p[5], 4, 6)
    f = comb(p[3], p[7], 4, 0)
    return comb(comb(c, d, 2, 3), comb(e, f, 2, 0), 1, 0)


ROUTE_BLOCK = 128
ROUTE_LAG = 9
RING = 3


def _peer_body(x_ref, x2f_ref, x2_ref, nfw2_ref, wqt_ref, keys_ref, nfw_ref, nlw_ref, uv_ref, o_ref,
               gbuf0, gbuf1, sem, hn_scr, cs_scr, acc, hb_scr, it_scr, gt_scr, ring_i, ring_g, sidx, ssem, *, tb):
    i = pl.program_id(0)
    n = pl.num_programs(0)
    per_blk = ROUTE_BLOCK // tb
    n_units = (n // per_blk) * PEER_HEADS

    def route_scores(u, xsrc_ref):
        h = u % PEER_HEADS

        @pl.when(h == 0)
        def _():
            hb_scr[...] = _rms(xsrc_ref[...], nfw2_ref[...]).astype(BF16)

        r0 = pl.multiple_of(h * 2 * PEER_HALF, 2 * PEER_HALF)
        qt = lax.dot_general(wqt_ref[pl.ds(r0, 2 * PEER_HALF), :], hb_scr[...], (((1,), (1,)), ((), ())),
                             preferred_element_type=F32).astype(BF16)
        return [jnp.dot(keys_ref[2 * h], qt[0:PEER_HALF], preferred_element_type=F32),
                jnp.dot(keys_ref[2 * h + 1], qt[PEER_HALF:2 * PEER_HALF], preferred_element_type=F32)]

    def route_pairs(v, ix):
        cand, ecand = _pair_candidates(v[0], ix[0], v[1], ix[1])
        return _top16([cand], [ecand])

    def route_store(u, sc, e):
        h = u % PEER_HEADS
        p = jnp.exp(sc[0] - jnp.max(sc[0], axis=0, keepdims=True))
        o = pl.multiple_of(h * PEER_TOPK, PEER_TOPK)
        it_scr[pl.ds(o, PEER_TOPK), :] = e[0]
        gt_scr[pl.ds(o, PEER_TOPK), :] = p / jnp.sum(p, axis=0, keepdims=True)

        @pl.when(h == PEER_HEADS - 1)
        def _():
            slot = (u // PEER_HEADS) % RING
            ring_i[slot] = it_scr[...].T.astype(I32)
            ring_g[slot] = gt_scr[...].T

    def route_unit(u, xsrc_ref):
        route_store(u, *route_pairs(*_top16(route_scores(u, xsrc_ref))))

    def idx_copy(blk16, par):
        b = jnp.minimum(blk16, n - 1)
        r0 = pl.multiple_of((b % per_blk) * tb, tb)
        return pltpu.make_async_copy(ring_i.at[(b // per_blk) % RING, pl.ds(r0, tb), :], sidx.at[par], ssem.at[par])

    def wait_token(buf, b, t):
        rows = buf.at[pl.ds(t * PEER_HK, PEER_HK)]
        pltpu.make_async_copy(rows, rows, sem.at[b, t]).wait()

    @pl.when(i == 0)
    def _():
        def body(u, c):
            route_unit(u, x2f_ref)
            return c
        lax.fori_loop(0, PEER_HEADS, body, 0)
        for u in range(PEER_HEADS, ROUTE_LAG):
            route_unit(u, x2_ref)
        cp = idx_copy(0, 0)
        cp.start()
        cp.wait()

        def rows(t, c):
            for k in range(PEER_HK):
                pltpu.make_async_copy(uv_ref.at[sidx[0, t, k]], gbuf0.at[t * PEER_HK + k],
                                      sem.at[0, t]).start(priority=k % 2)
            return c
        lax.fori_loop(0, tb, rows, 0)

    shape = (ROWGRP, 8, LANES)
    eye = lax.broadcasted_iota(I32, shape, 2) == (lax.broadcasted_iota(I32, shape, 1) * ROWGRP
                                                  + lax.broadcasted_iota(I32, shape, 0))
    g_slot = (i // per_blk) % RING
    g_row0 = pl.multiple_of((i % per_blk) * tb, tb)

    def step(cur, cb, nxt, nb):
        nxt_idx = idx_copy(i + 1, nb)
        nxt_idx.start()
        x = x_ref[...]
        hn_scr[...] = _rms3(x, nfw_ref[...])
        nxt_idx.wait()

        u = jnp.minimum(i + ROUTE_LAG, n_units - 1)
        quarter = tb // 4
        stage = None
        for t in range(tb):
            if t == quarter:
                stage = route_scores(u, x2_ref)
            elif t == 2 * quarter:
                stage = _top16(stage)
            elif t == 3 * quarter:
                stage = route_pairs(*stage)
            wait_token(cur, cb, t)
            for k in range(PEER_HK):
                pltpu.make_async_copy(uv_ref.at[sidx[nb, t, k]], nxt.at[t * PEER_HK + k],
                                      sem.at[nb, t]).start(priority=k % 2)
            base = t * PEER_HK
            hn = hn_scr[t]
            prods = [cur[pl.ds(base + j * ROWGRP, ROWGRP), 0:NCH, :] * hn for j in range(8)]
            a = jnp.sum(_sublane_totals(prods), axis=-1, keepdims=True)
            gcol = jnp.sum(jnp.where(eye, ring_g[g_slot, pl.ds(g_row0 + t, 1), :], 0.0), axis=-1, keepdims=True)
            cc = gcol * (0.5 * a * (1.0 + lax.erf(a * (2.0 ** -0.5))))
            cs_scr[...] = jnp.broadcast_to(cc, shape)
            parts = [None] * 4
            for k in range(PEER_HK):
                j, g = divmod(k, ROWGRP)
                term = cs_scr[g, pl.ds(j, 1), :] * cur[base + k, NCH:2 * NCH, :]
                parts[k % 4] = term if parts[k % 4] is None else parts[k % 4] + term
            acc[t] = (parts[0] + parts[1]) + (parts[2] + parts[3])
        o_ref[...] = _rms3(x + acc[...], nlw_ref[...])
        route_store(u, *stage)

    @pl.when(i % 2 == 0)
    def _():
        step(gbuf0, 0, gbuf1, 1)

    @pl.when(i % 2 == 1)
    def _():
        step(gbuf1, 1, gbuf0, 0)

    @pl.when(i == n - 1)
    def _():
        @pl.when(i % 2 == 0)
        def _():
            for t in range(tb):
                wait_token(gbuf1, 1, t)

        @pl.when(i % 2 == 1)
        def _():
            for t in range(tb):
                wait_token(gbuf0, 0, t)


def _peer(x1, nfw, nlw, wq_t, keys, uv3, tb=16):
    T = x1.shape[0]
    nblk = T // tb
    per_blk = ROUTE_BLOCK // tb
    n_rblk = T // ROUTE_BLOCK
    assert n_rblk >= 2 and T % ROUTE_BLOCK == 0
    tok = lambda m: (m, 0, 0)
    r3 = lambda a: a.reshape(1, NCH, LANES)
    full2 = lambda a: pl.BlockSpec(a.shape, lambda m: (0,) * a.ndim)
    return pl.pallas_call(
        functools.partial(_peer_body, tb=tb),
        grid=(nblk,),
        in_specs=[
            pl.BlockSpec((tb, NCH, LANES), tok),
            pl.BlockSpec((ROUTE_BLOCK, D_MODEL), lambda m: (0, 0)),
            pl.BlockSpec((ROUTE_BLOCK, D_MODEL),
                         lambda m: (jnp.minimum((m + ROUTE_LAG) // per_blk, n_rblk - 1), 0)),
            pl.BlockSpec((1, D_MODEL), lambda m: (0, 0)),
            full2(wq_t), full2(keys),
            pl.BlockSpec((1, NCH, LANES), lambda m: (0, 0, 0)),
            pl.BlockSpec((1, NCH, LANES), lambda m: (0, 0, 0)),
            pl.BlockSpec(memory_space=pl.ANY),
        ],
        out_specs=pl.BlockSpec((tb, NCH, LANES), tok),
        out_shape=jax.ShapeDtypeStruct((T, NCH, LANES), F32),
        scratch_shapes=[
            pltpu.VMEM((tb * PEER_HK, 2 * NCH, LANES), F32),
            pltpu.VMEM((tb * PEER_HK, 2 * NCH, LANES), F32),
            pltpu.SemaphoreType.DMA((2, tb)),
            pltpu.VMEM((tb, NCH, LANES), F32),
            pltpu.VMEM((ROWGRP, 8, LANES), F32),
            pltpu.VMEM((tb, NCH, LANES), F32),
            pltpu.VMEM((ROUTE_BLOCK, D_MODEL), BF16),
            pltpu.VMEM((PEER_HK, ROUTE_BLOCK), F32),
            pltpu.VMEM((PEER_HK, ROUTE_BLOCK), F32),
            pltpu.VMEM((RING, ROUTE_BLOCK, PEER_HK), I32),
            pltpu.VMEM((RING, ROUTE_BLOCK, PEER_HK), F32),
            pltpu.SMEM((2, tb, PEER_HK), I32),
            pltpu.SemaphoreType.DMA((2,)),
        ],
        compiler_params=_cparams("arbitrary"),
        name="peer",
    )(x1.reshape(T, NCH, LANES), x1, x1, nfw[None, :], wq_t, keys, r3(nfw), r3(nlw), uv3)


def _layer_weights(w_in, conv_w, conv_b, dt_bias, a_log, d_skip):
    D = D_MODEL
    q, k, v, z, xbc, dt, gates = jnp.split(
        w_in, [ATTN_WIDTH, ATTN_WIDTH + KV_WIDTH, ATTN_WIDTH + 2 * KV_WIDTH,
               ATTN_WIDTH + 2 * KV_WIDTH + SSD_WIDTH,
               ATTN_WIDTH + 2 * KV_WIDTH + SSD_WIDTH + SSD_WIDTH + 2 * BC_WIDTH,
               ATTN_WIDTH + 2 * KV_WIDTH + SSD_WIDTH + SSD_WIDTH + 2 * BC_WIDTH + SSD_HEADS], axis=1)
    dup = lambda t: jnp.concatenate([t.reshape(D, N_KV_HEADS, HEAD_DIM)] * 2, axis=-1).reshape(D, 2 * KV_WIDTH)
    pad = PROJ_COLS - (COL_DT + SSD_HEADS)
    w_perm = jnp.concatenate([z, xbc, gates, q, dup(k), dup(v), dt, jnp.zeros((D, pad), w_in.dtype)],
                             axis=1).astype(BF16)
    lanes_pad = lambda a: jnp.pad(a.reshape(1, SSD_HEADS), ((0, 0), (0, LANES - SSD_HEADS)))
    ssd_consts = [
        conv_w[:, :SSD_WIDTH], conv_w[:, SSD_WIDTH:SSD_WIDTH + BC_WIDTH], conv_w[:, SSD_WIDTH + BC_WIDTH:],
        conv_b[None, :SSD_WIDTH], conv_b[None, SSD_WIDTH:SSD_WIDTH + BC_WIDTH], conv_b[None, SSD_WIDTH + BC_WIDTH:],
        lanes_pad(dt_bias), lanes_pad(a_log),
        jnp.repeat(d_skip, SSD_HEAD_DIM)[None, :],
    ]
    return w_perm, ssd_consts


def _constants():
    lane = jnp.arange(LANES)
    hl = lane % HEAD_DIM
    half = ROT_DIM // 2
    freq = ROPE_THETA ** (-jnp.arange(0, ROT_DIM, 2, dtype=F32) / ROT_DIM)
    invf = jnp.where(hl < ROT_DIM, freq[hl % half], 0.0).astype(F32)[None, :]
    sgn = jnp.where(hl < half, -1.0, jnp.where(hl < ROT_DIM, 1.0, 0.0)).astype(F32)[None, :]
    hrow = jnp.arange(LANES)[:, None]
    e64 = (hrow == (jnp.arange(SSD_WIDTH)[None, :] // SSD_HEAD_DIM)).astype(BF16)
    e128 = (hrow == (jnp.arange(SSD_HEADS * LANES)[None, :] // LANES)).astype(BF16)
    tril = (jnp.arange(SSD_CHUNK)[:, None] >= jnp.arange(SSD_CHUNK)[None, :]).astype(BF16)
    return invf, sgn, e64, e128, tril


def kernel(x, positions, norm_mix_w, w_in, attn_sinks, conv_w, conv_b, dt_bias, a_log, d_skip, ssd_norm_w,
           w_attn_o, w_ssd_o, w_out, norm_ffn_w, peer_wq, peer_keys, peer_u, peer_v, norm_final_w):
    B, S, D = x.shape
    T = B * S
    assert w_in.shape[0] == 1, "single-layer block: the final norm is fused into the PEER kernel"
    l = 0
    invf, sgn, e64, e128, tril = _constants()
    pos_col = positions.reshape(T, 1)
    xf = x.reshape(T, D)
    w_perm, ssd_consts = _layer_weights(w_in[l], conv_w[l], conv_b[l], dt_bias[l], a_log[l], d_skip[l])
    proj = _inproj(xf, norm_mix_w[l][None, :], w_perm)
    attn_o = _attention(proj, pos_col, invf, sgn, attn_sinks[l], w_attn_o[l].astype(BF16), B, S)
    yn = _ssd(proj, ssd_consts + [ssd_norm_w[l][None, :], e64, e128, tril], B, S)
    x1 = _merge(xf, attn_o, yn, proj, w_ssd_o[l].astype(BF16), w_out[l].astype(BF16))
    keys = peer_keys[l].reshape(PEER_HEADS * 2, PEER_N_KEYS, PEER_HALF).astype(BF16)
    uv3 = jnp.concatenate([peer_u[l].reshape(-1, NCH, LANES), peer_v[l].reshape(-1, NCH, LANES)], axis=1)
    out = _peer(x1, norm_ffn_w[l], norm_final_w, peer_wq[l].T.astype(BF16), keys, uv3)
    return out.reshape(B, S, D)
```

```python
import functools
import math

import jax
import jax.numpy as jnp
from jax import lax
from jax.experimental import pallas as pl
from jax.experimental.pallas import tpu as pltpu

F32 = jnp.float32
BF16 = jnp.bfloat16
I32 = jnp.int32

D_MODEL = 1024
N_Q_HEADS = 16
N_KV_HEADS = 4
HEAD_DIM = 64
ATTN_WIDTH = N_Q_HEADS * HEAD_DIM
KV_WIDTH = N_KV_HEADS * HEAD_DIM
WINDOW = 128
ROT_DIM = HEAD_DIM // 4
ROPE_THETA = 500000.0
SSD_WIDTH = 2 * D_MODEL
SSD_HEAD_DIM = 64
SSD_HEADS = SSD_WIDTH // SSD_HEAD_DIM
SSD_GROUPS = 4
SSD_STATE = 128
SSD_CONV = 4
SSD_CHUNK = 128
BC_WIDTH = SSD_GROUPS * SSD_STATE
PEER_HEADS = 8
PEER_N_KEYS = 128
PEER_TOPK = 16
PEER_HALF = 128
PEER_HK = PEER_HEADS * PEER_TOPK
EPS = 1e-6

LANES = 128
VMEM_LIMIT = 56 * 1024 * 1024

COL_Z = 0
COL_XS = COL_Z + SSD_WIDTH
COL_BM = COL_XS + SSD_WIDTH
COL_CM = COL_BM + BC_WIDTH
COL_GA = COL_CM + BC_WIDTH
COL_GS = COL_GA + D_MODEL
COL_Q = COL_GS + D_MODEL
COL_K = COL_Q + ATTN_WIDTH
COL_V = COL_K + 2 * KV_WIDTH
COL_DT = COL_V + 2 * KV_WIDTH
PROJ_TN = 2432
PROJ_COLS = 4 * PROJ_TN
assert COL_DT + LANES <= PROJ_COLS


def _cparams(*sem):
    return pltpu.CompilerParams(dimension_semantics=sem, vmem_limit_bytes=VMEM_LIMIT)


def _rms(x, w):
    return x * lax.rsqrt(jnp.mean(x * x, axis=-1, keepdims=True) + EPS) * w


def _inproj_body(x_ref, nw_ref, w_ref, o_ref):
    h = _rms(x_ref[...], nw_ref[...])
    o_ref[...] = jnp.dot(h.astype(BF16), w_ref[...], preferred_element_type=F32)


def _inproj(xf, norm_w, w_perm, tm=512):
    T = xf.shape[0]
    return pl.pallas_call(
        _inproj_body,
        grid=(PROJ_COLS // PROJ_TN, T // tm),
        in_specs=[
            pl.BlockSpec((tm, D_MODEL), lambda n, m: (m, 0)),
            pl.BlockSpec((1, D_MODEL), lambda n, m: (0, 0)),
            pl.BlockSpec((D_MODEL, PROJ_TN), lambda n, m: (0, n)),
        ],
        out_specs=pl.BlockSpec((tm, PROJ_TN), lambda n, m: (m, n)),
        out_shape=jax.ShapeDtypeStruct((T, PROJ_COLS), F32),
        compiler_params=_cparams("arbitrary", "arbitrary"),
        name="inproj",
    )(xf, norm_w, w_perm)


def _attn_body(q_ref, kc_ref, vc_ref, kp_ref, vp_ref, pos_ref, posp_ref, invf_ref, sgn_ref,
               sink_ref, wo_ref, o_ref, *, tq):
    i = pl.program_id(1)
    lane = lax.broadcasted_iota(I32, (1, LANES), 1)
    lo8 = (lane % HEAD_DIM) < (ROT_DIM // 2)
    mlo = lane < HEAD_DIM

    def cos_sin(pos):
        ang = pos.astype(F32) * invf_ref[...]
        return jnp.cos(ang), jnp.sin(ang) * sgn_ref[...]

    def rope(t, cs_sn):
        cs, sn = cs_sn
        outs = []
        for j in range(t.shape[1] // LANES):
            tj = t[:, j * LANES:(j + 1) * LANES]
            sh = jnp.where(lo8, pltpu.roll(tj, LANES - ROT_DIM // 2, 1), pltpu.roll(tj, ROT_DIM // 2, 1))
            outs.append(tj * cs + sh * sn)
        return jnp.concatenate(outs, axis=1)

    trig = cos_sin(pos_ref[...])
    qrot = rope(q_ref[...], trig)
    kfull = jnp.concatenate([rope(kp_ref[...], cos_sin(posp_ref[...])), rope(kc_ref[...], trig)],
                            axis=0).astype(BF16)
    vfull = jnp.concatenate([vp_ref[...], vc_ref[...]], axis=0).astype(BF16)

    qi = lax.broadcasted_iota(I32, (WINDOW, 2 * WINDOW), 0)
    kj = lax.broadcasted_iota(I32, (WINDOW, 2 * WINDOW), 1)
    rel = WINDOW + qi - kj
    band = (rel >= 0) & (rel < WINDOW)
    rgrp = lax.broadcasted_iota(I32, (4 * WINDOW, 1), 0) // WINDOW
    scale = HEAD_DIM ** -0.5

    for c in range(tq // WINDOW):
        valid = band
        if c == 0:
            valid = band & ((kj >= WINDOW) | (i > 0))
        valid4 = jnp.concatenate([valid] * 4, axis=0)
        tiles = []
        for h in range(N_KV_HEADS):
            kh = kfull[c * WINDOW:c * WINDOW + 2 * WINDOW, h * LANES:(h + 1) * LANES]
            vh = vfull[c * WINDOW:c * WINDOW + 2 * WINDOW, h * LANES:(h + 1) * LANES]
            t0 = qrot[c * WINDOW:(c + 1) * WINDOW, (2 * h) * LANES:(2 * h + 1) * LANES]
            t1 = qrot[c * WINDOW:(c + 1) * WINDOW, (2 * h + 1) * LANES:(2 * h + 2) * LANES]
            qg = jnp.concatenate([jnp.where(mlo, t0, 0.0), jnp.where(mlo, 0.0, t0),
                                  jnp.where(mlo, t1, 0.0), jnp.where(mlo, 0.0, t1)], axis=0).astype(BF16)
            s = lax.dot_general(qg, kh, (((1,), (1,)), ((), ())), preferred_element_type=F32) * scale
            s = jnp.where(valid4, s, -jnp.inf)
            sk = jnp.where(rgrp == 0, sink_ref[4 * h],
                           jnp.where(rgrp == 1, sink_ref[4 * h + 1],
                                     jnp.where(rgrp == 2, sink_ref[4 * h + 2], sink_ref[4 * h + 3])))
            m = jnp.maximum(jnp.max(s, axis=-1, keepdims=True), sk)
            p = jnp.exp(s - m)
            den = jnp.sum(p, axis=-1, keepdims=True) + jnp.exp(sk - m)
            o = jnp.dot(p.astype(BF16), vh, preferred_element_type=F32) / den
            tiles.append(jnp.where(mlo, o[0:WINDOW], o[WINDOW:2 * WINDOW]))
            tiles.append(jnp.where(mlo, o[2 * WINDOW:3 * WINDOW], o[3 * WINDOW:4 * WINDOW]))
        attn = jnp.concatenate(tiles, axis=1).astype(BF16)
        o_ref[c * WINDOW:(c + 1) * WINDOW, :] = jnp.dot(attn, wo_ref[...], preferred_element_type=F32)


def _attention(proj, pos_col, invf, sgn, sinks, w_o, B, S, tq=512):
    T = B * S
    nq = S // tq
    nb = S // WINDOW
    r = tq // WINDOW
    cur = lambda col: (lambda b, i: (b * nq + i, col))
    prev = lambda col: (lambda b, i: (b * nb + jnp.maximum(i * r - 1, 0), col))
    return pl.pallas_call(
        functools.partial(_attn_body, tq=tq),
        grid=(B, nq),
        in_specs=[
            pl.BlockSpec((tq, ATTN_WIDTH), cur(COL_Q // ATTN_WIDTH)),
            pl.BlockSpec((tq, 2 * KV_WIDTH), cur(COL_K // (2 * KV_WIDTH))),
            pl.BlockSpec((tq, 2 * KV_WIDTH), cur(COL_V // (2 * KV_WIDTH))),
            pl.BlockSpec((WINDOW, 2 * KV_WIDTH), prev(COL_K // (2 * KV_WIDTH))),
            pl.BlockSpec((WINDOW, 2 * KV_WIDTH), prev(COL_V // (2 * KV_WIDTH))),
            pl.BlockSpec((tq, 1), cur(0)),
            pl.BlockSpec((WINDOW, 1), prev(0)),
            pl.BlockSpec((1, LANES), lambda b, i: (0, 0)),
            pl.BlockSpec((1, LANES), lambda b, i: (0, 0)),
            pl.BlockSpec(memory_space=pltpu.SMEM),
            pl.BlockSpec((ATTN_WIDTH, D_MODEL), lambda b, i: (0, 0)),
        ],
        out_specs=pl.BlockSpec((tq, D_MODEL), cur(0)),
        out_shape=jax.ShapeDtypeStruct((T, D_MODEL), F32),
        compiler_params=_cparams("arbitrary", "arbitrary"),
        name="attention",
    )(proj, proj, proj, proj, proj, pos_col, pos_col, invf, sgn, sinks, w_o)


def _split3(a):
    hi = a.astype(BF16)
    r1 = a - hi.astype(F32)
    mid = r1.astype(BF16)
    lo = (r1 - mid.astype(F32)).astype(BF16)
    return hi, mid, lo


def _sel_dot(a, e):
    hi, mid, lo = _split3(a)
    d = lambda u: jnp.dot(u, e, preferred_element_type=F32)
    return d(hi) + d(mid) + d(lo)


def _ssd_body(z_ref, xs_ref, bm_ref, cm_ref, dt_ref, cwx_ref, cwb_ref, cwc_ref, cbx_ref, cbb_ref, cbc_ref,
              dtb_ref, alog_ref, dsk_ref, nw_ref, e64_ref, e128_ref, tril_ref, o_ref,
              state, tail_x, tail_b, tail_c):
    L = SSD_CHUNK

    @pl.when(pl.program_id(1) == 0)
    def _():
        state[...] = jnp.zeros_like(state)
        tail_x[...] = jnp.zeros_like(tail_x)
        tail_b[...] = jnp.zeros_like(tail_b)
        tail_c[...] = jnp.zeros_like(tail_c)

    row8 = lax.broadcasted_iota(I32, (8, 1), 0)

    def conv_silu(u, tail_ref, w_ref, b_ref):
        tail = tail_ref[...]
        acc = u * w_ref[SSD_CONV - 1:SSD_CONV, :] + b_ref[...]
        for j in range(1, SSD_CONV):
            ru = pltpu.roll(u, j, 0)
            head = jnp.where(row8 < j, pltpu.roll(tail, j, 0), ru[0:8])
            sh = jnp.concatenate([head, ru[8:]], axis=0)
            acc = acc + sh * w_ref[SSD_CONV - 1 - j:SSD_CONV - j, :]
        tail_ref[...] = u[L - 8:L]
        return acc * jax.nn.sigmoid(acc)

    xs = conv_silu(xs_ref[...], tail_x, cwx_ref, cbx_ref)
    bm = conv_silu(bm_ref[...], tail_b, cwb_ref, cbb_ref)
    cm = conv_silu(cm_ref[...], tail_c, cwc_ref, cbc_ref)

    dt = jax.nn.softplus(dt_ref[...] + dtb_ref[...])
    dA = dt * (-jnp.exp(alog_ref[...]))
    hi, mid, lo = _split3(dA)
    tril = tril_ref[...]
    cs = lambda u: jnp.dot(tril, u, preferred_element_type=F32)
    acum = cs(hi) + cs(mid) + cs(lo)
    acum_t = acum.T
    e64 = e64_ref[...]
    dt_x = _sel_dot(dt, e64)
    acum_x = _sel_dot(acum, e64)
    tot_x = acum_x[L - 1:L, :]
    xdt = xs * dt_x
    xdt_b = xdt.astype(BF16)
    wx_b = (jnp.exp(tot_x - acum_x) * xdt).astype(BF16)
    eac_x = jnp.exp(acum_x)
    etot_x = jnp.exp(tot_x)
    cm_b = cm.astype(BF16)
    bm_b = bm.astype(BF16)

    ti = lax.broadcasted_iota(I32, (L, L), 0)
    si = lax.broadcasted_iota(I32, (L, L), 1)
    causal = ti >= si
    lane = lax.broadcasted_iota(I32, (1, LANES), 1)
    mlo = lane < SSD_HEAD_DIM
    GW = SSD_WIDTH // SSD_GROUPS
    ys = []
    for g in range(SSD_GROUPS):
        cmg = cm_b[:, g * SSD_STATE:(g + 1) * SSD_STATE]
        bmg = bm_b[:, g * SSD_STATE:(g + 1) * SSD_STATE]
        cb = lax.dot_general(cmg, bmg, (((1,), (1,)), ((), ())), preferred_element_type=F32)
        colx = _sel_dot(acum, e128_ref[:, g * 8 * LANES:(g + 1) * 8 * LANES])
        tiles = []
        for j in range(4):
            xt = xdt_b[:, g * GW + j * LANES:g * GW + (j + 1) * LANES]
            acc = None
            for half in range(2):
                r = 2 * j + half
                h = 8 * g + r
                seg = colx[:, r * LANES:(r + 1) * LANES] - acum_t[h:h + 1, :]
                dec = jnp.exp(jnp.where(causal, seg, -jnp.inf))
                mm = (cb * dec).astype(BF16)
                xm = jnp.where(mlo, xt, 0.0) if half == 0 else jnp.where(mlo, 0.0, xt)
                y = jnp.dot(mm, xm.astype(BF16), preferred_element_type=F32)
                acc = y if acc is None else acc + y
            tiles.append(acc)
        y_intra = jnp.concatenate(tiles, axis=1)
        st = state[:, g * GW:(g + 1) * GW]
        y_inter = jnp.dot(cmg, st.astype(BF16), preferred_element_type=F32) * eac_x[:, g * GW:(g + 1) * GW]
        ys.append(y_intra + y_inter)
        bmt = bm[:, g * SSD_STATE:(g + 1) * SSD_STATE].T.astype(BF16)
        state[:, g * GW:(g + 1) * GW] = st * etot_x[:, g * GW:(g + 1) * GW] + jnp.dot(
            bmt, wx_b[:, g * GW:(g + 1) * GW], preferred_element_type=F32)

    y = jnp.concatenate(ys, axis=1) + dsk_ref[...] * xs
    z = z_ref[...]
    y = y * (z * jax.nn.sigmoid(z))
    outs = []
    for g in range(SSD_GROUPS):
        outs.append(_rms(y[:, g * GW:(g + 1) * GW], nw_ref[:, g * GW:(g + 1) * GW]))
    o_ref[...] = jnp.concatenate(outs, axis=1).astype(BF16)


def _ssd(proj, consts, B, S):
    T = B * S
    L = SSD_CHUNK
    nc = S // L
    blk = lambda w, col: pl.BlockSpec((L, w), lambda b, c: (b * nc + c, col))
    full = lambda a: pl.BlockSpec(a.shape, lambda b, c: (0,) * a.ndim)
    return pl.pallas_call(
        _ssd_body,
        grid=(B, nc),
        in_specs=[
            blk(SSD_WIDTH, COL_Z // SSD_WIDTH),
            blk(SSD_WIDTH, COL_XS // SSD_WIDTH),
            blk(BC_WIDTH, COL_BM // BC_WIDTH),
            blk(BC_WIDTH, COL_CM // BC_WIDTH),
            blk(LANES, COL_DT // LANES),
        ] + [full(a) for a in consts],
        out_specs=pl.BlockSpec((L, SSD_WIDTH), lambda b, c: (b * nc + c, 0)),
        out_shape=jax.ShapeDtypeStruct((T, SSD_WIDTH), BF16),
        scratch_shapes=[
            pltpu.VMEM((SSD_STATE, SSD_WIDTH), F32),
            pltpu.VMEM((8, SSD_WIDTH), F32),
            pltpu.VMEM((8, BC_WIDTH), F32),
            pltpu.VMEM((8, BC_WIDTH), F32),
        ],
        compiler_params=_cparams("arbitrary", "arbitrary"),
        name="ssd",
    )(proj, proj, proj, proj, proj, *consts)


def _merge_body(x_ref, at_ref, yn_ref, ga_ref, gs_ref, wso_ref, wout_ref, o_ref):
    ssd = jnp.dot(yn_ref[...], wso_ref[...], preferred_element_type=F32)
    m = jax.nn.sigmoid(ga_ref[...]) * at_ref[...] + jax.nn.sigmoid(gs_ref[...]) * ssd
    o_ref[...] = x_ref[...] + jnp.dot(m.astype(BF16), wout_ref[...], preferred_element_type=F32)


def _merge(xf, attn_o, yn, proj, w_ssd_o, w_out, tm=512):
    T = xf.shape[0]
    row = lambda w, col: pl.BlockSpec((tm, w), lambda m: (m, col))
    return pl.pallas_call(
        _merge_body,
        grid=(T // tm,),
        in_specs=[
            row(D_MODEL, 0), row(D_MODEL, 0), row(SSD_WIDTH, 0),
            row(D_MODEL, COL_GA // D_MODEL), row(D_MODEL, COL_GS // D_MODEL),
            pl.BlockSpec((SSD_WIDTH, D_MODEL), lambda m: (0, 0)),
            pl.BlockSpec((D_MODEL, D_MODEL), lambda m: (0, 0)),
        ],
        out_specs=row(D_MODEL, 0),
        out_shape=jax.ShapeDtypeStruct((T, D_MODEL), F32),
        compiler_params=_cparams("arbitrary"),
        name="merge",
    )(xf, attn_o, yn, proj, proj, w_ssd_o, w_out)


def _top16(ss, payloads=None):
    n = ss[0].shape[0]
    iota = lax.broadcasted_iota(I32, ss[0].shape, 0).astype(F32)
    ss = list(ss)
    vals = [[] for _ in ss]
    sel = [[] for _ in ss]
    for _ in range(PEER_TOPK):
        for c in range(len(ss)):
            s = ss[c]
            m = jnp.max(s, axis=0, keepdims=True)
            tied = jnp.where(s == m, iota, float(n))
            am = jnp.min(tied, axis=0, keepdims=True)
            hit = tied == am
            vals[c].append(m)
            sel[c].append(am if payloads is None
                          else jnp.max(jnp.where(hit, payloads[c], -1.0), axis=0, keepdims=True))
            ss[c] = jnp.where(hit, -jnp.inf, s)
    cat = lambda rows: jnp.concatenate(rows, axis=0)
    return [cat(v) for v in vals], [cat(v) for v in sel]


def _pair_candidates(v1, i1, v2, i2):
    cands, ecands = [], []
    for i in range(PEER_TOPK):
        nj = PEER_TOPK // (i + 1)
        njp = 8 * ((nj + 7) // 8)
        cv = v1[i:i + 1] + v2[0:njp]
        if njp > nj:
            cv = jnp.where(lax.broadcasted_iota(I32, cv.shape, 0) < nj, cv, -jnp.inf)
        cands.append(cv)
        ecands.append(i1[i:i + 1] * PEER_N_KEYS + i2[0:njp])
    return jnp.concatenate(cands, axis=0), jnp.concatenate(ecands, axis=0)


NCH = D_MODEL // LANES
ROWGRP = PEER_HK // 8


def _rms3(x, w):
    ms = jnp.sum(jnp.sum(x * x, axis=2, keepdims=True), axis=1, keepdims=True) * (1.0 / D_MODEL)
    return x * lax.rsqrt(ms + EPS) * w


def _sublane_totals(p):
    sub = lax.broadcasted_iota(I32, p[0].shape, 1)

    def comb(a, b, h, phase):
        m = ((sub - phase) & (2 * h - 1)) < h
        return jnp.where(m, a, b) + pltpu.roll(jnp.where(m, b, a), h, 1)

    c = comb(p[0], p[4], 4, 5)
    d = comb(p[2], p[6], 4, 7)
    e = comb(p[1], p[5], 4, 6)
    f = comb(p[3], p[7], 4, 0)
    return comb(comb(c, d, 2, 3), comb(e, f, 2, 0), 1, 0)


ROUTE_BLOCK = 128
ROUTE_LAG = 9
RING = 3


def _peer_body(x_ref, x2f_ref, x2_ref, nfw2_ref, wqt_ref, keys_ref, nfw_ref, nlw_ref, uv_ref, o_ref,
               gbuf0, gbuf1, sem, hn_scr, cs_scr, acc, hb_scr, it_scr, gt_scr, ring_i, ring_g, sidx, ssem, *, tb):
    i = pl.program_id(0)
    n = pl.num_programs(0)
    per_blk = ROUTE_BLOCK // tb
    n_units = (n // per_blk) * PEER_HEADS

    def route_scores(u, xsrc_ref):
        h = u % PEER_HEADS

        @pl.when(h == 0)
        def _():
            hb_scr[...] = _rms(xsrc_ref[...], nfw2_ref[...]).astype(BF16)

        r0 = pl.multiple_of(h * 2 * PEER_HALF, 2 * PEER_HALF)
        qt = lax.dot_general(wqt_ref[pl.ds(r0, 2 * PEER_HALF), :], hb_scr[...], (((1,), (1,)), ((), ())),
                             preferred_element_type=F32).astype(BF16)
        return [jnp.dot(keys_ref[2 * h], qt[0:PEER_HALF], preferred_element_type=F32),
                jnp.dot(keys_ref[2 * h + 1], qt[PEER_HALF:2 * PEER_HALF], preferred_element_type=F32)]

    def route_pairs(v, ix):
        cand, ecand = _pair_candidates(v[0], ix[0], v[1], ix[1])
        return _top16([cand], [ecand])

    def route_store(u, sc, e):
        h = u % PEER_HEADS
        p = jnp.exp(sc[0] - jnp.max(sc[0], axis=0, keepdims=True))
        o = pl.multiple_of(h * PEER_TOPK, PEER_TOPK)
        it_scr[pl.ds(o, PEER_TOPK), :] = e[0]
        gt_scr[pl.ds(o, PEER_TOPK), :] = p / jnp.sum(p, axis=0, keepdims=True)

        @pl.when(h == PEER_HEADS - 1)
        def _():
            slot = (u // PEER_HEADS) % RING
            ring_i[slot] = it_scr[...].T.astype(I32)
            ring_g[slot] = gt_scr[...].T

    def route_unit(u, xsrc_ref):
        route_store(u, *route_pairs(*_top16(route_scores(u, xsrc_ref))))

    def idx_copy(blk16, par):
        b = jnp.minimum(blk16, n - 1)
        r0 = pl.multiple_of((b % per_blk) * tb, tb)
        return pltpu.make_async_copy(ring_i.at[(b // per_blk) % RING, pl.ds(r0, tb), :], sidx.at[par], ssem.at[par])

    def wait_token(buf, b, t):
        rows = buf.at[pl.ds(t * PEER_HK, PEER_HK)]
        pltpu.make_async_copy(rows, rows, sem.at[b, t]).wait()

    @pl.when(i == 0)
    def _():
        def body(u, c):
            route_unit(u, x2f_ref)
            return c
        lax.fori_loop(0, PEER_HEADS, body, 0)
        for u in range(PEER_HEADS, ROUTE_LAG):
            route_unit(u, x2_ref)
        cp = idx_copy(0, 0)
        cp.start()
        cp.wait()

        def rows(t, c):
            for k in range(PEER_HK):
                pltpu.make_async_copy(uv_ref.at[sidx[0, t, k]], gbuf0.at[t * PEER_HK + k],
                                      sem.at[0, t]).start(priority=k % 2)
            return c
        lax.fori_loop(0, tb, rows, 0)

    shape = (ROWGRP, 8, LANES)
    eye = lax.broadcasted_iota(I32, shape, 2) == (lax.broadcasted_iota(I32, shape, 1) * ROWGRP
                                                  + lax.broadcasted_iota(I32, shape, 0))
    g_slot = (i // per_blk) % RING
    g_row0 = pl.multiple_of((i % per_blk) * tb, tb)

    def step(cur, cb, nxt, nb):
        nxt_idx = idx_copy(i + 1, nb)
        nxt_idx.start()
        x = x_ref[...]
        hn_scr[...] = _rms3(x, nfw_ref[...])
        nxt_idx.wait()

        u = jnp.minimum(i + ROUTE_LAG, n_units - 1)
        quarter = tb // 4
        stage = None
        for t in range(tb):
            if t == quarter:
                stage = route_scores(u, x2_ref)
            elif t == 2 * quarter:
                stage = _top16(stage)
            elif t == 3 * quarter:
                stage = route_pairs(*stage)
            wait_token(cur, cb, t)
            for k in range(PEER_HK):
                pltpu.make_async_copy(uv_ref.at[sidx[nb, t, k]], nxt.at[t * PEER_HK + k],
                                      sem.at[nb, t]).start(priority=k % 2)
            base = t * PEER_HK
            hn = hn_scr[t]
            prods = [cur[pl.ds(base + j * ROWGRP, ROWGRP), 0:NCH, :] * hn for j in range(8)]
            a = jnp.sum(_sublane_totals(prods), axis=-1, keepdims=True)
            gcol = jnp.sum(jnp.where(eye, ring_g[g_slot, pl.ds(g_row0 + t, 1), :], 0.0), axis=-1, keepdims=True)
            cc = gcol * (0.5 * a * (1.0 + lax.erf(a * (2.0 ** -0.5))))
            cs_scr[...] = jnp.broadcast_to(cc, shape)
            parts = [None] * 4
            for k in range(PEER_HK):
                j, g = divmod(k, ROWGRP)
                term = cs_scr[g, pl.ds(j, 1), :] * cur[base + k, NCH:2 * NCH, :]
                parts[k % 4] = term if parts[k % 4] is None else parts[k % 4] + term
            acc[t] = (parts[0] + parts[1]) + (parts[2] + parts[3])
        y = _rms3(x + acc[...], nlw_ref[...])
        for c in range(NCH):
            o_ref[:, c * LANES:(c + 1) * LANES] = y[:, c, :]
        route_store(u, *stage)

    @pl.when(i % 2 == 0)
    def _():
        step(gbuf0, 0, gbuf1, 1)

    @pl.when(i % 2 == 1)
    def _():
        step(gbuf1, 1, gbuf0, 0)

    @pl.when(i == n - 1)
    def _():
        @pl.when(i % 2 == 0)
        def _():
            for t in range(tb):
                wait_token(gbuf1, 1, t)

        @pl.when(i % 2 == 1)
        def _():
            for t in range(tb):
                wait_token(gbuf0, 0, t)


def _peer(x1, nfw, nlw, wq_t, keys, uv3, tb=16):
    T = x1.shape[0]
    nblk = T // tb
    per_blk = ROUTE_BLOCK // tb
    n_rblk = T // ROUTE_BLOCK
    assert n_rblk >= 2 and T % ROUTE_BLOCK == 0
    tok = lambda m: (m, 0, 0)
    r3 = lambda a: a.reshape(1, NCH, LANES)
    full2 = lambda a: pl.BlockSpec(a.shape, lambda m: (0,) * a.ndim)
    return pl.pallas_call(
        functools.partial(_peer_body, tb=tb),
        grid=(nblk,),
        in_specs=[
            pl.BlockSpec((tb, NCH, LANES), tok),
            pl.BlockSpec((ROUTE_BLOCK, D_MODEL), lambda m: (0, 0)),
            pl.BlockSpec((ROUTE_BLOCK, D_MODEL),
                         lambda m: (jnp.minimum((m + ROUTE_LAG) // per_blk, n_rblk - 1), 0)),
            pl.BlockSpec((1, D_MODEL), lambda m: (0, 0)),
            full2(wq_t), full2(keys),
            pl.BlockSpec((1, NCH, LANES), lambda m: (0, 0, 0)),
            pl.BlockSpec((1, NCH, LANES), lambda m: (0, 0, 0)),
            pl.BlockSpec(memory_space=pl.ANY),
        ],
        out_specs=pl.BlockSpec((tb, D_MODEL), lambda m: (m, 0)),
        out_shape=jax.ShapeDtypeStruct((T, D_MODEL), F32),
        scratch_shapes=[
            pltpu.VMEM((tb * PEER_HK, 2 * NCH, LANES), F32),
            pltpu.VMEM((tb * PEER_HK, 2 * NCH, LANES), F32),
            pltpu.SemaphoreType.DMA((2, tb)),
            pltpu.VMEM((tb, NCH, LANES), F32),
            pltpu.VMEM((ROWGRP, 8, LANES), F32),
            pltpu.VMEM((tb, NCH, LANES), F32),
            pltpu.VMEM((ROUTE_BLOCK, D_MODEL), BF16),
            pltpu.VMEM((PEER_HK, ROUTE_BLOCK), F32),
            pltpu.VMEM((PEER_HK, ROUTE_BLOCK), F32),
            pltpu.VMEM((RING, ROUTE_BLOCK, PEER_HK), I32),
            pltpu.VMEM((RING, ROUTE_BLOCK, PEER_HK), F32),
            pltpu.SMEM((2, tb, PEER_HK), I32),
            pltpu.SemaphoreType.DMA((2,)),
        ],
        compiler_params=_cparams("arbitrary"),
        name="peer",
    )(x1.reshape(T, NCH, LANES), x1, x1, nfw[None, :], wq_t, keys, r3(nfw), r3(nlw), uv3)


def _layer_weights(w_in, conv_w, conv_b, dt_bias, a_log, d_skip):
    D = D_MODEL
    q, k, v, z, xbc, dt, gates = jnp.split(
        w_in, [ATTN_WIDTH, ATTN_WIDTH + KV_WIDTH, ATTN_WIDTH + 2 * KV_WIDTH,
               ATTN_WIDTH + 2 * KV_WIDTH + SSD_WIDTH,
               ATTN_WIDTH + 2 * KV_WIDTH + SSD_WIDTH + SSD_WIDTH + 2 * BC_WIDTH,
               ATTN_WIDTH + 2 * KV_WIDTH + SSD_WIDTH + SSD_WIDTH + 2 * BC_WIDTH + SSD_HEADS], axis=1)
    dup = lambda t: jnp.concatenate([t.reshape(D, N_KV_HEADS, HEAD_DIM)] * 2, axis=-1).reshape(D, 2 * KV_WIDTH)
    pad = PROJ_COLS - (COL_DT + SSD_HEADS)
    w_perm = jnp.concatenate([z, xbc, gates, q, dup(k), dup(v), dt, jnp.zeros((D, pad), w_in.dtype)],
                             axis=1).astype(BF16)
    lanes_pad = lambda a: jnp.pad(a.reshape(1, SSD_HEADS), ((0, 0), (0, LANES - SSD_HEADS)))
    ssd_consts = [
        conv_w[:, :SSD_WIDTH], conv_w[:, SSD_WIDTH:SSD_WIDTH + BC_WIDTH], conv_w[:, SSD_WIDTH + BC_WIDTH:],
        conv_b[None, :SSD_WIDTH], conv_b[None, SSD_WIDTH:SSD_WIDTH + BC_WIDTH], conv_b[None, SSD_WIDTH + BC_WIDTH:],
        lanes_pad(dt_bias), lanes_pad(a_log),
        jnp.repeat(d_skip, SSD_HEAD_DIM)[None, :],
    ]
    return w_perm, ssd_consts


def _constants():
    lane = jnp.arange(LANES)
    hl = lane % HEAD_DIM
    half = ROT_DIM // 2
    freq = ROPE_THETA ** (-jnp.arange(0, ROT_DIM, 2, dtype=F32) / ROT_DIM)
    invf = jnp.where(hl < ROT_DIM, freq[hl % half], 0.0).astype(F32)[None, :]
    sgn = jnp.where(hl < half, -1.0, jnp.where(hl < ROT_DIM, 1.0, 0.0)).astype(F32)[None, :]
    hrow = jnp.arange(LANES)[:, None]
    e64 = (hrow == (jnp.arange(SSD_WIDTH)[None, :] // SSD_HEAD_DIM)).astype(BF16)
    e128 = (hrow == (jnp.arange(SSD_HEADS * LANES)[None, :] // LANES)).astype(BF16)
    tril = (jnp.arange(SSD_CHUNK)[:, None] >= jnp.arange(SSD_CHUNK)[None, :]).astype(BF16)
    return invf, sgn, e64, e128, tril


def kernel(x, positions, norm_mix_w, w_in, attn_sinks, conv_w, conv_b, dt_bias, a_log, d_skip, ssd_norm_w,
           w_attn_o, w_ssd_o, w_out, norm_ffn_w, peer_wq, peer_keys, peer_u, peer_v, norm_final_w):
    B, S, D = x.shape
    T = B * S
    assert w_in.shape[0] == 1, "single-layer block: the final norm is fused into the PEER kernel"
    l = 0
    invf, sgn, e64, e128, tril = _constants()
    pos_col = positions.reshape(T, 1)
    xf = x.reshape(T, D)
    w_perm, ssd_consts = _layer_weights(w_in[l], conv_w[l], conv_b[l], dt_bias[l], a_log[l], d_skip[l])
    proj = _inproj(xf, norm_mix_w[l][None, :], w_perm)
    attn_o = _attention(proj, pos_col, invf, sgn, attn_sinks[l], w_attn_o[l].astype(BF16), B, S)
    yn = _ssd(proj, ssd_consts + [ssd_norm_w[l][None, :], e64, e128, tril], B, S)
    x1 = _merge(xf, attn_o, yn, proj, w_ssd_o[l].astype(BF16), w_out[l].astype(BF16))
    keys = peer_keys[l].reshape(PEER_HEADS * 2, PEER_N_KEYS, PEER_HALF).astype(BF16)
    uv3 = jnp.concatenate([peer_u[l].reshape(-1, NCH, LANES), peer_v[l].reshape(-1, NCH, LANES)], axis=1)
    out = _peer(x1, norm_ffn_w[l], norm_final_w, peer_wq[l].T.astype(BF16), keys, uv3)
    return out.reshape(B, S, D)
```
